```python
import math
import jax, jax.numpy as jnp
from jax import lax
import numpy as np

D_MODEL = 2048
BATCH = 2
SEQ = 8192
DEPTH = 1

CHUNK = 64
EPS = 1e-6
DA_HEADS = 8
DA_QK_DIM = 64
DA_V_DIM = 2 * DA_QK_DIM
DA_WIDTH = DA_HEADS * DA_V_DIM
Q_BLOCK = 128
ML_HEADS = 8
ML_HEAD_DIM = 128
ML_WIDTH = ML_HEADS * ML_HEAD_DIM
CONV_WIDTH = 4
PEER_HEADS = 8
N_KEYS = 128
N_EXPERTS = N_KEYS * N_KEYS
PEER_TOPK = 16
PEER_QDIM = 256
PEER_HALF = PEER_QDIM // 2
PEER_BLOCK = 128
IN_SIZES = (DA_WIDTH, DA_WIDTH, DA_WIDTH,
            ML_WIDTH, ML_WIDTH, ML_WIDTH, ML_WIDTH,
            2 * ML_HEADS,
            2 * D_MODEL)
IN_WIDTH = sum(IN_SIZES)
IN_SPLITS = tuple(int(s) for s in np.cumsum(IN_SIZES)[:-1])

kernel_name = "hybrid_diffattn_mlstm_peer_block"


def rmsnorm(x, g):
    xf = x.astype(jnp.float32)
    y = xf * lax.rsqrt(jnp.mean(xf * xf, axis=-1, keepdims=True) + EPS)
    return (y * g.astype(jnp.float32)).astype(x.dtype)


def diff_attention(q, k, v, lam, sub_g, lambda_init):
    B, S = q.shape[0], q.shape[1]
    q = jnp.transpose(q, (0, 2, 3, 1, 4)) * (DA_QK_DIM ** -0.5)
    k = jnp.transpose(k, (0, 2, 3, 1, 4))
    v = jnp.transpose(v, (0, 2, 1, 3))
    slopes = 2.0 ** (-8.0 * jnp.arange(1, DA_HEADS + 1, dtype=jnp.float32) / DA_HEADS)
    kpos = jnp.arange(S)
    n_blk = S // Q_BLOCK

    def block(i):
        start = i * Q_BLOCK
        qb = lax.dynamic_slice_in_dim(q, start, Q_BLOCK, axis=3)
        qpos = start + jnp.arange(Q_BLOCK)
        s = jnp.einsum('bhmqd,bhmkd->bhmqk', qb, k).astype(jnp.float32)
        dist = jnp.abs(qpos[:, None] - kpos[None, :]).astype(jnp.float32)
        allowed = (kpos[None, :] // CHUNK) <= (qpos[:, None] // CHUNK)
        bias = jnp.where(allowed[None], -slopes[:, None, None] * dist[None], -jnp.inf)
        p = jax.nn.softmax(s + bias[None, :, None], axis=-1)
        a = p[:, :, 0] - lam * p[:, :, 1]
        return jnp.einsum('bhqk,bhkd->bhqd', a.astype(v.dtype), v)

    o = lax.map(block, jnp.arange(n_blk))
    o = jnp.transpose(o, (1, 0, 3, 2, 4)).reshape(B, S, DA_HEADS, DA_V_DIM)
    o = rmsnorm(o, sub_g) * (1.0 - lambda_init)
    return o.reshape(B, S, DA_WIDTH)


def causal_conv(x, w, b):
    S = x.shape[1]
    xp = jnp.pad(x, ((0, 0), (CONV_WIDTH - 1, 0), (0, 0)))
    return sum(w[j] * xp[:, j:j + S] for j in range(CONV_WIDTH)) + b


def mlstm(q, k, v, i_raw, f_raw):
    B, S, H, d = q.shape
    nc = S // CHUNK
    f32 = jnp.float32

    def to_chunks(t):
        t = t.reshape((B, nc, CHUNK, H) + t.shape[3:])
        return jnp.moveaxis(jnp.moveaxis(t, 1, 0), 3, 2)

    qc = to_chunks(q.astype(f32))
    kc = to_chunks(k.astype(f32) * (d ** -0.5))
    vc = to_chunks(v.astype(f32))
    lic = to_chunks(i_raw.astype(f32))
    lfc = to_chunks(jax.nn.log_sigmoid(f_raw.astype(f32)))
    causal = jnp.tril(jnp.ones((CHUNK, CHUNK), dtype=bool))

    def step(carry, inp):
        C, n, m = carry
        qb, kb, vb, lib, lfb = inp
        b = jnp.cumsum(lfb, axis=-1)
        D = jnp.where(causal, b[..., :, None] - b[..., None, :] + lib[..., None, :], -jnp.inf)
        inter = b + m[..., None]
        m_t = jnp.maximum(jnp.max(D, axis=-1), inter)
        W = jnp.exp(D - m_t[..., None])
        sc = jnp.exp(inter - m_t)
        Wqk = W * jnp.einsum('bhtd,bhsd->bhts', qb, kb)
        num = sc[..., None] * jnp.einsum('bhtd,bhde->bhte', qb, C) + jnp.einsum('bhts,bhse->bhte', Wqk, vb)
        den = sc * jnp.einsum('bhtd,bhd->bht', qb, n) + jnp.sum(Wqk, axis=-1)
        den = jnp.maximum(jnp.abs(den), jnp.exp(-m_t))
        h = num / den[..., None]
        bL = b[..., -1]
        g = bL[..., None] - b + lib
        m_new = jnp.maximum(bL + m, jnp.max(g, axis=-1))
        decay = jnp.exp(bL + m - m_new)
        w = jnp.exp(g - m_new[..., None])
        C_new = decay[..., None, None] * C + jnp.einsum('bhs,bhsd,bhse->bhde', w, kb, vb)
        n_new = decay[..., None] * n + jnp.einsum('bhs,bhsd->bhd', w, kb)
        return (C_new, n_new, m_new), h

    init = (jnp.zeros((B, H, d, d), f32), jnp.zeros((B, H, d), f32), jnp.zeros((B, H), f32))
    _, hs = lax.scan(step, init, (qc, kc, vc, lic, lfc))
    return jnp.transpose(hs, (1, 0, 3, 2, 4)).reshape(B, S, H, d).astype(v.dtype)


def peer(x, w_query, sub_keys, expert_u, expert_v):
    B, S, D = x.shape
    T = B * S
    nb = T // PEER_BLOCK
    xb = x.reshape(nb, PEER_BLOCK, D)

    def block(xi):
        q = (xi @ w_query).reshape(PEER_BLOCK, PEER_HEADS, 2, PEER_HALF)
        s = jnp.einsum('thcd,hcnd->thcn', q, sub_keys).astype(jnp.float32)
        top_s, top_i = lax.top_k(s, PEER_TOPK)
        cand_s = (top_s[:, :, 0, :, None] + top_s[:, :, 1, None, :]).reshape(PEER_BLOCK, PEER_HEADS, PEER_TOPK * PEER_TOPK)
        cand_i = (top_i[:, :, 0, :, None] * N_KEYS + top_i[:, :, 1, None, :]).reshape(PEER_BLOCK, PEER_HEADS, PEER_TOPK * PEER_TOPK)
        best_s, pos = lax.top_k(cand_s, PEER_TOPK)
        idx = jnp.take_along_axis(cand_i, pos, axis=-1).reshape(PEER_BLOCK, PEER_HEADS * PEER_TOPK)
        g = jax.nn.softmax(best_s, axis=-1).reshape(PEER_BLOCK, PEER_HEADS * PEER_TOPK)
        u = expert_u[idx]
        a = jnp.einsum('pd,pkd->pk', xi, u).astype(jnp.float32)
        act = (jax.nn.gelu(a) * g).astype(xi.dtype)
        return jnp.einsum('pk,pkd->pd', act, expert_v[idx])

    return lax.map(block, xb).reshape(B, S, D)


def setup_inputs(seed: int = 0) -> dict:
    key = jax.random.key(seed)
    ks = jax.random.split(key, 20)
    f32 = jnp.float32
    nrm = lambda k, shape, s: jax.random.normal(k, shape, f32) * s
    x = jax.random.normal(ks[0], (BATCH, SEQ, D_MODEL), f32)
    norm_mix = 1.0 + nrm(ks[1], (DEPTH, D_MODEL), 0.02)
    w_in = nrm(ks[2], (DEPTH, D_MODEL, IN_WIDTH), D_MODEL ** -0.5)
    conv_w = nrm(ks[3], (DEPTH, CONV_WIDTH, 2 * ML_WIDTH), CONV_WIDTH ** -0.5)
    conv_b = nrm(ks[4], (DEPTH, 2 * ML_WIDTH), 0.02)
    gate_i_bias = nrm(ks[5], (DEPTH, ML_HEADS), 0.1)
    gate_f_bias = jnp.linspace(3.0, 6.0, ML_HEADS, dtype=f32)[None] + nrm(ks[6], (DEPTH, ML_HEADS), 0.1)
    ml_gate_bias = jnp.concatenate([gate_i_bias, gate_f_bias], axis=-1)
    da_lambda = nrm(ks[7], (DEPTH, 4, DA_QK_DIM), 0.1)
    da_subln = 1.0 + nrm(ks[8], (DEPTH, DA_V_DIM), 0.02)
    ml_headnorm = 1.0 + nrm(ks[9], (DEPTH, ML_WIDTH), 0.02)
    w_branch_a = nrm(ks[10], (DEPTH, DA_WIDTH, D_MODEL), DA_WIDTH ** -0.5)
    w_branch_b = nrm(ks[11], (DEPTH, ML_WIDTH, D_MODEL), ML_WIDTH ** -0.5)
    w_out = nrm(ks[12], (DEPTH, D_MODEL, D_MODEL), D_MODEL ** -0.5)
    norm_ffn = 1.0 + nrm(ks[13], (DEPTH, D_MODEL), 0.02)
    peer_query = nrm(ks[14], (DEPTH, D_MODEL, PEER_HEADS * PEER_QDIM), D_MODEL ** -0.5)
    peer_subkeys = nrm(ks[15], (DEPTH, PEER_HEADS, 2, N_KEYS, PEER_HALF), PEER_HALF ** -0.5)
    expert_u = nrm(ks[16], (DEPTH, N_EXPERTS, D_MODEL), D_MODEL ** -0.5)
    expert_v = nrm(ks[17], (DEPTH, N_EXPERTS, D_MODEL), 0.5)
    norm_final = 1.0 + nrm(ks[18], (D_MODEL,), 0.02)
    return {"x": x, "norm_mix": norm_mix, "w_in": w_in, "conv_w": conv_w, "conv_b": conv_b,
            "ml_gate_bias": ml_gate_bias, "da_lambda": da_lambda, "da_subln": da_subln,
            "ml_headnorm": ml_headnorm, "w_branch_a": w_branch_a, "w_branch_b": w_branch_b,
            "w_out": w_out, "norm_ffn": norm_ffn, "peer_query": peer_query,
            "peer_subkeys": peer_subkeys, "expert_u": expert_u, "expert_v": expert_v,
            "norm_final": norm_final}


def reference(x, norm_mix, w_in, conv_w, conv_b, ml_gate_bias, da_lambda, da_subln,
              ml_headnorm, w_branch_a, w_branch_b, w_out, norm_ffn, peer_query,
              peer_subkeys, expert_u, expert_v, norm_final):
    B, S, D = x.shape
    for l in range(DEPTH):
        lambda_init = 0.8 - 0.6 * math.exp(-0.3 * l)
        h = rmsnorm(x, norm_mix[l])
        proj = h @ w_in[l]
        da_q, da_k, da_v, ml_q, ml_k, ml_v, ml_o, ml_if, gates = jnp.split(proj, IN_SPLITS, axis=-1)
        lq = da_lambda[l].astype(jnp.float32)
        lam = jnp.exp(jnp.sum(lq[0] * lq[1])) - jnp.exp(jnp.sum(lq[2] * lq[3])) + lambda_init
        branch_a = diff_attention(da_q.reshape(B, S, DA_HEADS, 2, DA_QK_DIM),
                                  da_k.reshape(B, S, DA_HEADS, 2, DA_QK_DIM),
                                  da_v.reshape(B, S, DA_HEADS, DA_V_DIM),
                                  lam, da_subln[l], lambda_init)
        qk = jax.nn.silu(causal_conv(jnp.concatenate([ml_q, ml_k], axis=-1), conv_w[l], conv_b[l]))
        mq, mk = jnp.split(qk, 2, axis=-1)
        if_raw = ml_if + ml_gate_bias[l]
        hm = mlstm(mq.reshape(B, S, ML_HEADS, ML_HEAD_DIM), mk.reshape(B, S, ML_HEADS, ML_HEAD_DIM),
                   ml_v.reshape(B, S, ML_HEADS, ML_HEAD_DIM), if_raw[..., :ML_HEADS], if_raw[..., ML_HEADS:])
        hm = rmsnorm(hm, ml_headnorm[l].reshape(ML_HEADS, ML_HEAD_DIM)).reshape(B, S, ML_WIDTH)
        branch_b = jax.nn.sigmoid(ml_o) * hm
        gate_a, gate_b = jnp.split(gates, 2, axis=-1)
        merged = jax.nn.sigmoid(gate_a) * (branch_a @ w_branch_a[l]) + jax.nn.sigmoid(gate_b) * (branch_b @ w_branch_b[l])
        x = x + merged @ w_out[l]
        x = x + peer(rmsnorm(x, norm_ffn[l]), peer_query[l], peer_subkeys[l], expert_u[l], expert_v[l])
    return rmsnorm(x, norm_final)
```

```python
import functools

import jax
import jax.numpy as jnp
from jax import lax
from jax.experimental import pallas as pl
from jax.experimental.pallas import tpu as pltpu

F32 = jnp.float32
BF16 = jnp.bfloat16

CHUNK = 64
EPS = 1e-6
LAMBDA_INIT = 0.8 - 0.6
DA_HEADS = 8
DA_QK = 64
HEAD_W = 128
BRANCH_W = DA_HEADS * HEAD_W
ML_HEADS = 8
CONV_W = 4
PEER_HEADS = 8
N_KEYS = 128
TOPK = 16
N_PLAIN_COLS = 7 * BRANCH_W
N_GATE_LOGITS = 2 * ML_HEADS

LANES_V7X = 128
SUBLANES_V7X = 8
VMEM_LIMIT_V7X = 56 * 1024 * 1024

NEG_INF = float("-inf")


def _cparams(sem):
    return pltpu.CompilerParams(dimension_semantics=sem, vmem_limit_bytes=VMEM_LIMIT_V7X)


def _tiles(t_rows, seq, d_model):
    return dict(
        proj_tm=min(512, t_rows), proj_tn=512,
        attn_tq=min(256, seq),
        ml_len=min(256, seq),
        merge_tm=min(512, t_rows), merge_tn=min(512, d_model),
        out_tm=min(256, t_rows),
        peer_tm=min(512, t_rows), peer_eb=512,
    )


def _inproj_kernel(x_ref, g_ref, w_ref, wif_ref, proj_ref, if_ref, h_ref, *, n_plain):
    j = pl.program_id(1)

    @pl.when(j == 0)
    def _():
        x = x_ref[...]
        ms = jnp.mean(x * x, axis=-1, keepdims=True)
        h_ref[...] = (x * lax.rsqrt(ms + EPS) * g_ref[...]).astype(BF16)
        if_ref[...] = jnp.dot(h_ref[...], wif_ref[...], preferred_element_type=F32)

    acc = jnp.dot(h_ref[...], w_ref[...], preferred_element_type=F32)

    @pl.when(j < n_plain)
    def _():
        proj_ref[...] = acc.astype(BF16)

    @pl.when(j >= n_plain)
    def _():
        proj_ref[...] = jax.nn.sigmoid(acc).astype(BF16)


def _inproj(x2, g, w_cat, w_if, tm, tn):
    t_rows, d = x2.shape
    n = w_cat.shape[1]
    return pl.pallas_call(
        functools.partial(_inproj_kernel, n_plain=N_PLAIN_COLS // tn),
        grid=(t_rows // tm, n // tn),
        in_specs=[
            pl.BlockSpec((tm, d), lambda i, j: (i, 0)),
            pl.BlockSpec((1, d), lambda i, j: (0, 0)),
            pl.BlockSpec((d, tn), lambda i, j: (0, j)),
            pl.BlockSpec((d, LANES_V7X), lambda i, j: (0, 0)),
        ],
        out_specs=[
            pl.BlockSpec((tm, tn), lambda i, j: (i, j)),
            pl.BlockSpec((tm, LANES_V7X), lambda i, j: (i, 0)),
        ],
        out_shape=[
            jax.ShapeDtypeStruct((t_rows, n), BF16),
            jax.ShapeDtypeStruct((t_rows, LANES_V7X), F32),
        ],
        scratch_shapes=[pltpu.VMEM((tm, d), BF16)],
        compiler_params=_cparams(("parallel", "arbitrary")),
        name="inproj",
    )(x2, g, w_cat, w_if)


def _attn_kernel(slope_ref, lam_ref, subg_ref, q_ref, k_ref, v_ref, o_ref,
                 boff_ref, bdiag_ref, m_ref, l_ref, acc_ref, *, tq):
    qi = pl.program_id(2)
    slope = slope_ref[0:1, 0:1]
    rows = 2 * tq

    @pl.when(qi == 0)
    def _():
        r = lax.broadcasted_iota(jnp.int32, (rows, tq), 0)
        r = jnp.where(r >= tq, r - tq, r)
        c = lax.broadcasted_iota(jnp.int32, (rows, tq), 1)
        rel = (r - c).astype(F32)
        boff_ref[...] = -slope * rel
        allowed = (c // CHUNK) <= (r // CHUNK)
        bdiag_ref[...] = jnp.where(allowed, -slope * jnp.abs(rel), NEG_INF)

    q = q_ref[...] * jnp.asarray(DA_QK ** -0.5, BF16)
    lane = lax.broadcasted_iota(jnp.int32, (tq, HEAD_W), 1)
    zero = jnp.zeros_like(q)
    q2 = jnp.concatenate([jnp.where(lane < DA_QK, q, zero), jnp.where(lane >= DA_QK, q, zero)], axis=0)

    m_ref[...] = jnp.full((rows, 1), NEG_INF, F32)
    l_ref[...] = jnp.zeros((rows, 1), F32)
    acc_ref[...] = jnp.zeros((rows, HEAD_W), F32)

    def step(k0, bias, shift_const):
        k = k_ref[pl.ds(k0, tq), :]
        v = v_ref[pl.ds(k0, tq), :]
        s = lax.dot_general(q2, k, (((1,), (1,)), ((), ())), preferred_element_type=F32)
        t = s + bias
        m_old = m_ref[...]
        m_new = jnp.maximum(m_old, jnp.max(t, axis=-1, keepdims=True) + shift_const)
        p = jnp.exp(t - (m_new - shift_const))
        alpha = jnp.exp(m_old - m_new)
        l_ref[...] = alpha * l_ref[...] + jnp.sum(p, axis=-1, keepdims=True)
        acc_ref[...] = alpha * acc_ref[...] + jnp.dot(p.astype(BF16), v, preferred_element_type=F32)
        m_ref[...] = m_new

    def body(ki, carry):
        gap = ((qi - ki) * tq).astype(F32)
        step(pl.multiple_of(ki * tq, tq), boff_ref[...], -slope * gap)
        return carry

    lax.fori_loop(0, qi, body, 0)
    step(pl.multiple_of(qi * tq, tq), bdiag_ref[...], jnp.zeros((1, 1), F32))

    lq = lam_ref[...]
    lam = (jnp.exp(jnp.sum(lq[0:1] * lq[1:2], keepdims=True))
           - jnp.exp(jnp.sum(lq[2:3] * lq[3:4], keepdims=True)) + LAMBDA_INIT)
    o = acc_ref[...] / l_ref[...]
    a = o[:tq] - lam * o[tq:]
    ms = jnp.mean(a * a, axis=-1, keepdims=True)
    y = a * lax.rsqrt(ms + EPS) * subg_ref[...]
    o_ref[...] = (y * (1.0 - LAMBDA_INIT)).astype(BF16)


def _attention(proj, slopes, da_lambda, subg, batch, seq, tq):
    t_rows = proj.shape[0]
    nq = seq // tq
    return pl.pallas_call(
        functools.partial(_attn_kernel, tq=tq),
        grid=(batch, DA_HEADS, nq),
        in_specs=[
            pl.BlockSpec((None, 1, LANES_V7X), lambda b, h, i: (h, 0, 0)),
            pl.BlockSpec((4, DA_QK), lambda b, h, i: (0, 0)),
            pl.BlockSpec((1, HEAD_W), lambda b, h, i: (0, 0)),
            pl.BlockSpec((tq, HEAD_W), lambda b, h, i: (b * nq + i, h)),
            pl.BlockSpec((seq, HEAD_W), lambda b, h, i: (b, DA_HEADS + h)),
            pl.BlockSpec((seq, HEAD_W), lambda b, h, i: (b, 2 * DA_HEADS + h)),
        ],
        out_specs=pl.BlockSpec((tq, HEAD_W), lambda b, h, i: (b * nq + i, h)),
        out_shape=jax.ShapeDtypeStruct((t_rows, BRANCH_W), BF16),
        scratch_shapes=[
            pltpu.VMEM((2 * tq, tq), F32),
            pltpu.VMEM((2 * tq, tq), F32),
            pltpu.VMEM((2 * tq, 1), F32),
            pltpu.VMEM((2 * tq, 1), F32),
            pltpu.VMEM((2 * tq, HEAD_W), F32),
        ],
        compiler_params=_cparams(("parallel", "parallel", "arbitrary")),
        name="diff_attention",
    )(slopes, da_lambda, subg, proj, proj, proj)


def _mlstm_kernel(q_ref, k_ref, v_ref, o_ref, gate_ref, gbias_ref, cw_ref, cb_ref, hn_ref,
                  out_ref, c_ref, n_ref, m_ref, qext_ref, kext_ref, *, length):
    ci = pl.program_id(1)
    halo = SUBLANES_V7X

    @pl.when(ci == 0)
    def _():
        c_ref[...] = jnp.zeros_like(c_ref)
        n_ref[...] = jnp.zeros_like(n_ref)
        m_ref[...] = jnp.zeros_like(m_ref)
        qext_ref[0:halo, :] = jnp.zeros((halo, BRANCH_W), F32)
        kext_ref[0:halo, :] = jnp.zeros((halo, BRANCH_W), F32)

    def conv_silu(x_ref, ext_ref, col0):
        ext_ref[halo:, :] = x_ref[...].astype(F32)
        y = cb_ref[:, col0:col0 + BRANCH_W]
        for j in range(CONV_W):
            y = y + cw_ref[j:j + 1, col0:col0 + BRANCH_W] * ext_ref[pl.ds(halo - (CONV_W - 1) + j, length), :]
        ext_ref[0:halo, :] = ext_ref[length:length + halo, :]
        return y * jax.nn.sigmoid(y)

    qc = conv_silu(q_ref, qext_ref, 0).astype(BF16)
    kc = conv_silu(k_ref, kext_ref, BRANCH_W) * (HEAD_W ** -0.5)
    kc_b = kc.astype(BF16)

    g = gate_ref[...] + gbias_ref[...]
    lane = lax.broadcasted_iota(jnp.int32, (length, LANES_V7X), 1)
    lg = jnp.where(lane < ML_HEADS, g, jax.nn.log_sigmoid(g))
    lgt = lg.T
    r = lax.broadcasted_iota(jnp.int32, (length, length), 0)
    c = lax.broadcasted_iota(jnp.int32, (length, length), 1)
    causal = r >= c
    tril = causal.astype(F32)
    triu = (r <= c).astype(F32)
    hi = lax.Precision.HIGHEST
    bcol_all = jnp.dot(tril, lg, precision=hi, preferred_element_type=F32)
    brow_all = jnp.dot(lgt[ML_HEADS:2 * ML_HEADS], triu, precision=hi, preferred_element_type=F32)
    lirow_all = lgt[0:ML_HEADS]

    for h in range(ML_HEADS):
        cs = slice(h * HEAD_W, (h + 1) * HEAD_W)
        b_col = bcol_all[:, ML_HEADS + h:ML_HEADS + h + 1]
        li_col = lg[:, h:h + 1]
        b_row = brow_all[h:h + 1, :]
        li_row = lirow_all[h:h + 1, :]
        m_prev = m_ref[h:h + 1, 0:1]
        n_row = n_ref[h:h + 1, :]
        cmat = c_ref[h]
        qh = qc[:, cs]
        kh = kc[:, cs]
        vh = v_ref[:, cs]

        d = jnp.where(causal, b_col - b_row + li_row, NEG_INF)
        inter = b_col + m_prev
        m_t = jnp.maximum(jnp.max(d, axis=-1, keepdims=True), inter)
        w = jnp.exp(d - m_t)
        sc = jnp.exp(inter - m_t)
        s = lax.dot_general(qh, kc_b[:, cs], (((1,), (1,)), ((), ())), preferred_element_type=F32)
        wqk = w * s
        num = (sc * jnp.dot(qh, cmat.astype(BF16), preferred_element_type=F32)
               + jnp.dot(wqk.astype(BF16), vh, preferred_element_type=F32))
        den = (sc * jnp.sum(qh.astype(F32) * n_row, axis=-1, keepdims=True)
               + jnp.sum(wqk, axis=-1, keepdims=True))
        den = jnp.maximum(jnp.abs(den), jnp.exp(-m_t))
        hh = num / den

        b_last = b_col[length - 1:length, :]
        g_col = b_last - b_col + li_col
        m_new = jnp.maximum(b_last + m_prev, jnp.max(g_col, axis=0, keepdims=True))
        decay = jnp.exp(b_last + m_prev - m_new)
        kw = kh * jnp.exp(g_col - m_new)
        c_ref[h] = decay * cmat + lax.dot_general(
            kw.astype(BF16), vh, (((0,), (0,)), ((), ())), preferred_element_type=F32)
        n_ref[h:h + 1, :] = decay * n_row + jnp.sum(kw, axis=0, keepdims=True)
        m_ref[h:h + 1, :] = jnp.broadcast_to(m_new, (1, LANES_V7X))

        ms = jnp.mean(hh * hh, axis=-1, keepdims=True)
        hn = hh * lax.rsqrt(ms + EPS) * hn_ref[:, cs]
        out_ref[:, cs] = (jax.nn.sigmoid(o_ref[:, cs].astype(F32)) * hn).astype(BF16)


def _mlstm(proj, gates, gbias, conv_w, conv_b, headnorm, batch, seq, length):
    t_rows = proj.shape[0]
    nc = seq // length
    col = lambda k: pl.BlockSpec((length, BRANCH_W), lambda b, c: (b * nc + c, k))
    const = lambda shape: pl.BlockSpec(shape, lambda b, c: (0, 0))
    return pl.pallas_call(
        functools.partial(_mlstm_kernel, length=length),
        grid=(batch, nc),
        in_specs=[
            col(3), col(4), col(5), col(6),
            pl.BlockSpec((length, LANES_V7X), lambda b, c: (b * nc + c, 0)),
            const((1, LANES_V7X)),
            const((CONV_W, 2 * BRANCH_W)),
            const((1, 2 * BRANCH_W)),
            const((1, BRANCH_W)),
        ],
        out_specs=pl.BlockSpec((length, BRANCH_W), lambda b, c: (b * nc + c, 0)),
        out_shape=jax.ShapeDtypeStruct((t_rows, BRANCH_W), BF16),
        scratch_shapes=[
            pltpu.VMEM((ML_HEADS, HEAD_W, HEAD_W), F32),
            pltpu.VMEM((ML_HEADS, HEAD_W), F32),
            pltpu.VMEM((ML_HEADS, LANES_V7X), F32),
            pltpu.VMEM((SUBLANES_V7X + length, BRANCH_W), F32),
            pltpu.VMEM((SUBLANES_V7X + length, BRANCH_W), F32),
        ],
        compiler_params=_cparams(("parallel", "arbitrary")),
        name="mlstm",
    )(proj, proj, proj, proj, gates, gbias, conv_w, conv_b, headnorm)


def _merge_kernel(a_ref, b_ref, wa_ref, wb_ref, ga_ref, gb_ref, out_ref):
    pa = jnp.dot(a_ref[...], wa_ref[...], preferred_element_type=F32)
    pb = jnp.dot(b_ref[...], wb_ref[...], preferred_element_type=F32)
    out_ref[...] = (ga_ref[...].astype(F32) * pa + gb_ref[...].astype(F32) * pb).astype(BF16)


def _merge(branch_a, branch_b, wa, wb, proj, d_model, tm, tn):
    t_rows = branch_a.shape[0]
    ga0 = N_PLAIN_COLS // tn
    gb0 = (N_PLAIN_COLS + d_model) // tn
    return pl.pallas_call(
        _merge_kernel,
        grid=(t_rows // tm, d_model // tn),
        in_specs=[
            pl.BlockSpec((tm, BRANCH_W), lambda i, j: (i, 0)),
            pl.BlockSpec((tm, BRANCH_W), lambda i, j: (i, 0)),
            pl.BlockSpec((BRANCH_W, tn), lambda i, j: (0, j)),
            pl.BlockSpec((BRANCH_W, tn), lambda i, j: (0, j)),
            pl.BlockSpec((tm, tn), lambda i, j: (i, ga0 + j)),
            pl.BlockSpec((tm, tn), lambda i, j: (i, gb0 + j)),
        ],
        out_specs=pl.BlockSpec((tm, tn), lambda i, j: (i, j)),
        out_shape=jax.ShapeDtypeStruct((t_rows, d_model), BF16),
        compiler_params=_cparams(("parallel", "arbitrary")),
        name="merge",
    )(branch_a, branch_b, wa, wb, proj, proj)


def _outproj_kernel(x_ref, mg_ref, w_ref, g_ref, x1t_ref, xnt_ref):
    x1 = x_ref[...] + jnp.dot(mg_ref[...], w_ref[...], preferred_element_type=F32)
    ms = jnp.mean(x1 * x1, axis=-1, keepdims=True)
    xn = x1 * lax.rsqrt(ms + EPS) * g_ref[...]
    x1t_ref[...] = x1.T
    xnt_ref[...] = xn.T.astype(BF16)


def _outproj(x2, merged, w_out, g, tm):
    t_rows, d = x2.shape
    return pl.pallas_call(
        _outproj_kernel,
        grid=(t_rows // tm,),
        in_specs=[
            pl.BlockSpec((tm, d), lambda i: (i, 0)),
            pl.BlockSpec((tm, d), lambda i: (i, 0)),
            pl.BlockSpec((d, d), lambda i: (0, 0)),
            pl.BlockSpec((1, d), lambda i: (0, 0)),
        ],
        out_specs=[
            pl.BlockSpec((d, tm), lambda i: (0, i)),
            pl.BlockSpec((d, tm), lambda i: (0, i)),
        ],
        out_shape=[
            jax.ShapeDtypeStruct((d, t_rows), F32),
            jax.ShapeDtypeStruct((d, t_rows), BF16),
        ],
        compiler_params=_cparams(("parallel",)),
        name="outproj",
    )(x2, merged, w_out, g)


def _top16(s):
    nk, tm = s.shape
    idx = lax.broadcasted_iota(jnp.int32, (nk, tm), 0)
    rank = jnp.zeros((nk, tm), F32)
    cur = s
    vals = []
    for r in range(TOPK):
        mx = jnp.max(cur, axis=0, keepdims=True)
        first = jnp.min(jnp.where(cur == mx, idx, nk), axis=0, keepdims=True)
        sel = idx == first
        rank = jnp.where(sel, float(r + 1), rank)
        cur = jnp.where(sel, NEG_INF, cur)
        vals.append(mx)
    return rank, jnp.concatenate(vals, axis=0)


def _staircase(a, b):
    tm = a.shape[1]
    ridx = lax.broadcasted_iota(jnp.int32, (TOPK, tm), 0)
    top = a[0:1] + b[0:1]
    cnt = jnp.zeros((TOPK, tm), jnp.int32)
    z = jnp.zeros((1, tm), F32)
    for _ in range(TOPK):
        nxt = jnp.full((TOPK, tm), NEG_INF, F32)
        for c in range(TOPK):
            nxt = jnp.where(cnt == c, b[c:c + 1], nxt)
        f = a + nxt
        mx = jnp.max(f, axis=0, keepdims=True)
        first = jnp.min(jnp.where(f == mx, ridx, TOPK), axis=0, keepdims=True)
        cnt = cnt + (ridx == first).astype(jnp.int32)
        z = z + jnp.exp(mx - top)
    return cnt.astype(F32), z


def _peer_rank_kernel(xnt_ref, wq_ref, sk_ref, e0_ref, n0_ref, e1_ref, r1_ref, qt_ref):
    qt_ref[...] = jnp.dot(wq_ref[...], xnt_ref[...], preferred_element_type=F32).astype(BF16)
    for h in range(PEER_HEADS):
        base = h * 2 * N_KEYS
        s0 = jnp.dot(sk_ref[h, 0], qt_ref[base:base + N_KEYS, :], preferred_element_type=F32)
        s1 = jnp.dot(sk_ref[h, 1], qt_ref[base + N_KEYS:base + 2 * N_KEYS, :], preferred_element_type=F32)
        r0, a = _top16(s0)
        r1, b = _top16(s1)
        cnt, z = _staircase(a, b)
        n0 = jnp.zeros_like(r0)
        for r in range(TOPK):
            n0 = jnp.where(r0 == float(r + 1), cnt[r:r + 1], n0)
        e0_ref[h] = jnp.exp(s0 - a[0:1]) / z
        n0_ref[h] = n0
        e1_ref[h] = jnp.exp(s1 - b[0:1])
        r1_ref[h] = jnp.where(r1 == 0.0, float(2 * TOPK), r1)


def _peer_rank(xnt, wq_t, subkeys, tm):
    d, t_rows = xnt.shape
    qd = wq_t.shape[0]
    head_arr = jax.ShapeDtypeStruct((PEER_HEADS, N_KEYS, t_rows), F32)
    head_spec = pl.BlockSpec((PEER_HEADS, N_KEYS, tm), lambda i: (0, 0, i))
    return pl.pallas_call(
        _peer_rank_kernel,
        grid=(t_rows // tm,),
        in_specs=[
            pl.BlockSpec((d, tm), lambda i: (0, i)),
            pl.BlockSpec((qd, d), lambda i: (0, 0)),
            pl.BlockSpec((PEER_HEADS, 2, N_KEYS, N_KEYS), lambda i: (0, 0, 0, 0)),
        ],
        out_specs=[head_spec] * 4,
        out_shape=[head_arr] * 4,
        scratch_shapes=[pltpu.VMEM((qd, tm), BF16)],
        compiler_params=_cparams(("parallel",)),
        name="peer_rank",
    )(xnt, wq_t, subkeys)


def _peer_mix_kernel(xnt_ref, x1t_ref, u_ref, vt_ref, e0_ref, n0_ref, e1_ref, r1_ref, g_ref,
                     y_ref, acc_ref, act_ref, *, eb):
    ei = pl.program_id(1)
    n_i = eb // N_KEYS

    @pl.when(ei == 0)
    def _():
        acc_ref[...] = jnp.zeros_like(acc_ref)

    a = jnp.dot(u_ref[...], xnt_ref[...], preferred_element_type=F32)
    for ii in range(n_i):
        i = ei * n_i + ii
        wgt = jnp.zeros((N_KEYS, a.shape[1]), F32)
        for h in range(PEER_HEADS):
            n_row = n0_ref[h, pl.ds(i, 1), :]
            e_row = e0_ref[h, pl.ds(i, 1), :]
            wgt = wgt + e_row * jnp.where(r1_ref[h] <= n_row, e1_ref[h], 0.0)
        rows = slice(ii * N_KEYS, (ii + 1) * N_KEYS)
        act_ref[rows, :] = (jax.nn.gelu(a[rows, :]) * wgt).astype(BF16)
    acc_ref[...] += jnp.dot(vt_ref[...], act_ref[...], preferred_element_type=F32)

    @pl.when(ei == pl.num_programs(1) - 1)
    def _():
        x2 = (x1t_ref[...] + acc_ref[...]).T
        ms = jnp.mean(x2 * x2, axis=-1, keepdims=True)
        y_ref[...] = x2 * lax.rsqrt(ms + EPS) * g_ref[...]


def _peer_mix(xnt, x1t, u, vt, e0, n0, e1, r1, g, tm, eb):
    d, t_rows = xnt.shape
    n_exp = u.shape[0]
    head_spec = pl.BlockSpec((PEER_HEADS, N_KEYS, tm), lambda t, e: (0, 0, t))
    return pl.pallas_call(
        functools.partial(_peer_mix_kernel, eb=eb),
        grid=(t_rows // tm, n_exp // eb),
        in_specs=[
            pl.BlockSpec((d, tm), lambda t, e: (0, t)),
            pl.BlockSpec((d, tm), lambda t, e: (0, t)),
            pl.BlockSpec((eb, d), lambda t, e: (e, 0)),
            pl.BlockSpec((d, eb), lambda t, e: (0, e)),
            head_spec, head_spec, head_spec, head_spec,
            pl.BlockSpec((1, d), lambda t, e: (0, 0)),
        ],
        out_specs=pl.BlockSpec((tm, d), lambda t, e: (t, 0)),
        out_shape=jax.ShapeDtypeStruct((t_rows, d), F32),
        scratch_shapes=[pltpu.VMEM((d, tm), F32), pltpu.VMEM((eb, tm), BF16)],
        compiler_params=_cparams(("parallel", "arbitrary")),
        name="peer_mix",
    )(xnt, x1t, u, vt, e0, n0, e1, r1, g)


def kernel(x, norm_mix, w_in, conv_w, conv_b, ml_gate_bias, da_lambda, da_subln, ml_headnorm,
           w_branch_a, w_branch_b, w_out, norm_ffn, peer_query, peer_subkeys, expert_u, expert_v,
           norm_final):
    batch, seq, d = x.shape
    t_rows = batch * seq
    assert norm_mix.shape[0] == 1, "single-layer trunk"
    assert seq % CHUNK == 0 and d % LANES_V7X == 0
    ts = _tiles(t_rows, seq, d)
    x2 = x.reshape(t_rows, d)

    w = w_in[0]
    gate0 = N_PLAIN_COLS + N_GATE_LOGITS
    w_cat = jnp.concatenate([w[:, :N_PLAIN_COLS], w[:, gate0:]], axis=1).astype(BF16)
    w_if = jnp.pad(w[:, N_PLAIN_COLS:gate0], ((0, 0), (0, LANES_V7X - N_GATE_LOGITS))).astype(BF16)
    gbias = jnp.pad(ml_gate_bias[0], (0, LANES_V7X - N_GATE_LOGITS)).reshape(1, LANES_V7X)
    slopes = 2.0 ** (-8.0 * jnp.arange(1, DA_HEADS + 1, dtype=F32) / DA_HEADS)
    slopes = jnp.broadcast_to(slopes[:, None, None], (DA_HEADS, 1, LANES_V7X))

    proj, gates = _inproj(x2, norm_mix[0].reshape(1, d), w_cat, w_if, ts["proj_tm"], ts["proj_tn"])
    branch_a = _attention(proj, slopes, da_lambda[0], da_subln[0].reshape(1, HEAD_W), batch, seq, ts["attn_tq"])
    branch_b = _mlstm(proj, gates, gbias, conv_w[0], conv_b[0].reshape(1, -1),
                      ml_headnorm[0].reshape(1, BRANCH_W), batch, seq, ts["ml_len"])
    merged = _merge(branch_a, branch_b, w_branch_a[0].astype(BF16), w_branch_b[0].astype(BF16),
                    proj, d, ts["merge_tm"], ts["merge_tn"])
    x1t, xnt = _outproj(x2, merged, w_out[0].astype(BF16), norm_ffn[0].reshape(1, d), ts["out_tm"])
    e0, n0, e1, r1 = _peer_rank(xnt, peer_query[0].T.astype(BF16), peer_subkeys[0].astype(BF16), ts["peer_tm"])
    y = _peer_mix(xnt, x1t, expert_u[0].astype(BF16), expert_v[0].T.astype(BF16), e0, n0, e1, r1,
                  norm_final.reshape(1, d), ts["peer_tm"], ts["peer_eb"])
    return y.reshape(batch, seq, d)
```

```python
import functools

import jax
import jax.numpy as jnp
from jax import lax
from jax.experimental import pallas as pl
from jax.experimental.pallas import tpu as pltpu

F32 = jnp.float32
BF16 = jnp.bfloat16

CHUNK = 64
EPS = 1e-6
LAMBDA_INIT = 0.8 - 0.6
DA_HEADS = 8
DA_QK = 64
HEAD_W = 128
BRANCH_W = DA_HEADS * HEAD_W
ML_HEADS = 8
CONV_W = 4
PEER_HEADS = 8
N_KEYS = 128
TOPK = 16
N_PLAIN_COLS = 7 * BRANCH_W
N_GATE_LOGITS = 2 * ML_HEADS

LANES_V7X = 128
SUBLANES_V7X = 8
VMEM_LIMIT_V7X = 56 * 1024 * 1024

NEG_INF = float("-inf")


def _cparams(sem):
    return pltpu.CompilerParams(dimension_semantics=sem, vmem_limit_bytes=VMEM_LIMIT_V7X)


def _tiles(t_rows, seq, d_model):
    return dict(
        proj_tm=min(1024, t_rows), proj_tn=1024 if d_model % 1024 == 0 else 512,
        attn_tq=min(512, seq),
        ml_len=min(256, seq),
        merge_tm=min(512, t_rows), merge_tn=min(512, d_model),
        out_tm=min(256, t_rows),
        peer_tm=min(512, t_rows), peer_eb=512,
    )


def _inproj_kernel(x_ref, g_ref, w_ref, wif_ref, proj_ref, if_ref, h_ref, *, n_plain):
    j = pl.program_id(1)

    @pl.when(j == 0)
    def _():
        x = x_ref[...]
        ms = jnp.mean(x * x, axis=-1, keepdims=True)
        h_ref[...] = (x * lax.rsqrt(ms + EPS) * g_ref[...]).astype(BF16)
        if_ref[...] = jnp.dot(h_ref[...], wif_ref[...], preferred_element_type=F32)

    acc = jnp.dot(h_ref[...], w_ref[...], preferred_element_type=F32)

    @pl.when(j < n_plain)
    def _():
        proj_ref[...] = acc.astype(BF16)

    @pl.when(j >= n_plain)
    def _():
        proj_ref[...] = jax.nn.sigmoid(acc).astype(BF16)


def _inproj(x2, g, w_cat, w_if, tm, tn):
    t_rows, d = x2.shape
    n = w_cat.shape[1]
    return pl.pallas_call(
        functools.partial(_inproj_kernel, n_plain=N_PLAIN_COLS // tn),
        grid=(t_rows // tm, n // tn),
        in_specs=[
            pl.BlockSpec((tm, d), lambda i, j: (i, 0)),
            pl.BlockSpec((1, d), lambda i, j: (0, 0)),
            pl.BlockSpec((d, tn), lambda i, j: (0, j)),
            pl.BlockSpec((d, LANES_V7X), lambda i, j: (0, 0)),
        ],
        out_specs=[
            pl.BlockSpec((tm, tn), lambda i, j: (i, j)),
            pl.BlockSpec((tm, LANES_V7X), lambda i, j: (i, 0)),
        ],
        out_shape=[
            jax.ShapeDtypeStruct((t_rows, n), BF16),
            jax.ShapeDtypeStruct((t_rows, LANES_V7X), F32),
        ],
        scratch_shapes=[pltpu.VMEM((tm, d), BF16)],
        compiler_params=_cparams(("parallel", "arbitrary")),
        name="inproj",
    )(x2, g, w_cat, w_if)


def _attn_kernel(slope_ref, lam_ref, subg_ref, q_ref, k_ref, v_ref, o_ref,
                 boff_ref, bdiag_ref, m_ref, l_ref, acc_ref, *, tq):
    qi = pl.program_id(2)
    slope = slope_ref[0:1, 0:1]
    cols = 2 * tq

    @pl.when(qi == 0)
    def _():
        c = lax.broadcasted_iota(jnp.int32, (tq, cols), 0)
        r = lax.broadcasted_iota(jnp.int32, (tq, cols), 1)
        r = jnp.where(r >= tq, r - tq, r)
        rel = (r - c).astype(F32)
        boff_ref[...] = -slope * rel
        allowed = (c // CHUNK) <= (r // CHUNK)
        bdiag_ref[...] = jnp.where(allowed, -slope * jnp.abs(rel), NEG_INF)

    q = q_ref[...] * jnp.asarray(DA_QK ** -0.5, BF16)
    lane = lax.broadcasted_iota(jnp.int32, (tq, HEAD_W), 1)
    zero = jnp.zeros_like(q)
    q2 = jnp.concatenate([jnp.where(lane < DA_QK, q, zero), jnp.where(lane >= DA_QK, q, zero)], axis=0)

    m_ref[...] = jnp.full((1, cols), NEG_INF, F32)
    l_ref[...] = jnp.zeros((1, cols), F32)
    acc_ref[...] = jnp.zeros((HEAD_W, cols), F32)

    def step(k0, bias, shift_const):
        k = k_ref[pl.ds(k0, tq), :]
        v = v_ref[pl.ds(k0, tq), :]
        s = lax.dot_general(k, q2, (((1,), (1,)), ((), ())), preferred_element_type=F32)
        t = s + bias
        m_old = m_ref[...]
        m_new = jnp.maximum(m_old, jnp.max(t, axis=0, keepdims=True) + shift_const)
        p = jnp.exp(t - (m_new - shift_const))
        alpha = jnp.exp(m_old - m_new)
        l_ref[...] = alpha * l_ref[...] + jnp.sum(p, axis=0, keepdims=True)
        pv = lax.dot_general(v, p.astype(BF16), (((0,), (0,)), ((), ())), preferred_element_type=F32)
        acc_ref[...] = alpha * acc_ref[...] + pv
        m_ref[...] = m_new

    def body(ki, carry):
        gap = ((qi - ki) * tq).astype(F32)
        step(pl.multiple_of(ki * tq, tq), boff_ref[...], -slope * gap)
        return carry

    lax.fori_loop(0, qi, body, 0)
    step(pl.multiple_of(qi * tq, tq), bdiag_ref[...], jnp.zeros((1, 1), F32))

    lq = lam_ref[...]
    lam = (jnp.exp(jnp.sum(lq[0:1] * lq[1:2], keepdims=True))
           - jnp.exp(jnp.sum(lq[2:3] * lq[3:4], keepdims=True)) + LAMBDA_INIT)
    o = acc_ref[...] / l_ref[...]
    a = o[:, :tq] - lam * o[:, tq:]
    ms = jnp.mean(a * a, axis=0, keepdims=True)
    y = a * lax.rsqrt(ms + EPS) * subg_ref[...]
    o_ref[...] = (y * (1.0 - LAMBDA_INIT)).T.astype(BF16)


def _attention(proj, slopes, da_lambda, subg, batch, seq, tq):
    t_rows = proj.shape[0]
    nq = seq // tq
    return pl.pallas_call(
        functools.partial(_attn_kernel, tq=tq),
        grid=(batch, DA_HEADS, nq),
        in_specs=[
            pl.BlockSpec((None, 1, LANES_V7X), lambda b, h, i: (h, 0, 0)),
            pl.BlockSpec((4, DA_QK), lambda b, h, i: (0, 0)),
            pl.BlockSpec((HEAD_W, 1), lambda b, h, i: (0, 0)),
            pl.BlockSpec((tq, HEAD_W), lambda b, h, i: (b * nq + i, h)),
            pl.BlockSpec((seq, HEAD_W), lambda b, h, i: (b, DA_HEADS + h)),
            pl.BlockSpec((seq, HEAD_W), lambda b, h, i: (b, 2 * DA_HEADS + h)),
        ],
        out_specs=pl.BlockSpec((tq, HEAD_W), lambda b, h, i: (b * nq + i, h)),
        out_shape=jax.ShapeDtypeStruct((t_rows, BRANCH_W), BF16),
        scratch_shapes=[
            pltpu.VMEM((tq, 2 * tq), F32),
            pltpu.VMEM((tq, 2 * tq), F32),
            pltpu.VMEM((1, 2 * tq), F32),
            pltpu.VMEM((1, 2 * tq), F32),
            pltpu.VMEM((HEAD_W, 2 * tq), F32),
        ],
        compiler_params=_cparams(("parallel", "parallel", "arbitrary")),
        name="diff_attention",
    )(slopes, da_lambda, subg, proj, proj, proj)


def _mlstm_kernel(q_ref, k_ref, v_ref, o_ref, gate_ref, gbias_ref, cw_ref, cb_ref, hn_ref,
                  out_ref, c_ref, n_ref, m_ref, qext_ref, kext_ref, *, length):
    ci = pl.program_id(1)
    halo = SUBLANES_V7X

    @pl.when(ci == 0)
    def _():
        c_ref[...] = jnp.zeros_like(c_ref)
        n_ref[...] = jnp.zeros_like(n_ref)
        m_ref[...] = jnp.zeros_like(m_ref)
        qext_ref[0:halo, :] = jnp.zeros((halo, BRANCH_W), F32)
        kext_ref[0:halo, :] = jnp.zeros((halo, BRANCH_W), F32)

    def conv_silu(x_ref, ext_ref, col0):
        ext_ref[halo:, :] = x_ref[...].astype(F32)
        y = cb_ref[:, col0:col0 + BRANCH_W]
        for j in range(CONV_W):
            y = y + cw_ref[j:j + 1, col0:col0 + BRANCH_W] * ext_ref[pl.ds(halo - (CONV_W - 1) + j, length), :]
        ext_ref[0:halo, :] = ext_ref[length:length + halo, :]
        return y * jax.nn.sigmoid(y)

    qc = conv_silu(q_ref, qext_ref, 0).astype(BF16)
    kc = conv_silu(k_ref, kext_ref, BRANCH_W) * (HEAD_W ** -0.5)
    kc_b = kc.astype(BF16)

    g = gate_ref[...] + gbias_ref[...]
    lane = lax.broadcasted_iota(jnp.int32, (length, LANES_V7X), 1)
    lg = jnp.where(lane < ML_HEADS, g, jax.nn.log_sigmoid(g))
    lgt = lg.T
    r = lax.broadcasted_iota(jnp.int32, (length, length), 0)
    c = lax.broadcasted_iota(jnp.int32, (length, length), 1)
    causal = r >= c
    tril = causal.astype(F32)
    triu = (r <= c).astype(F32)
    hi = lax.Precision.HIGHEST
    bcol_all = jnp.dot(tril, lg, precision=hi, preferred_element_type=F32)
    brow_all = jnp.dot(lgt[ML_HEADS:2 * ML_HEADS], triu, precision=hi, preferred_element_type=F32)
    lirow_all = lgt[0:ML_HEADS]

    for h in range(ML_HEADS):
        cs = slice(h * HEAD_W, (h + 1) * HEAD_W)
        b_col = bcol_all[:, ML_HEADS + h:ML_HEADS + h + 1]
        li_col = lg[:, h:h + 1]
        b_row = brow_all[h:h + 1, :]
        li_row = lirow_all[h:h + 1, :]
        m_prev = m_ref[h:h + 1, 0:1]
        n_row = n_ref[h:h + 1, :]
        cmat = c_ref[h]
        qh = qc[:, cs]
        kh = kc[:, cs]
        vh = v_ref[:, cs]

        d = jnp.where(causal, b_col - b_row + li_row, NEG_INF)
        inter = b_col + m_prev
        m_t = jnp.maximum(jnp.max(d, axis=-1, keepdims=True), inter)
        w = jnp.exp(d - m_t)
        sc = jnp.exp(inter - m_t)
        s = lax.dot_general(qh, kc_b[:, cs], (((1,), (1,)), ((), ())), preferred_element_type=F32)
        wqk = w * s
        num = (sc * jnp.dot(qh, cmat.astype(BF16), preferred_element_type=F32)
               + jnp.dot(wqk.astype(BF16), vh, preferred_element_type=F32))
        den = (sc * jnp.sum(qh.astype(F32) * n_row, axis=-1, keepdims=True)
               + jnp.sum(wqk, axis=-1, keepdims=True))
        den = jnp.maximum(jnp.abs(den), jnp.exp(-m_t))
        hh = num / den

        b_last = b_col[length - 1:length, :]
        g_col = b_last - b_col + li_col
        m_new = jnp.maximum(b_last + m_prev, jnp.max(g_col, axis=0, keepdims=True))
        decay = jnp.exp(b_last + m_prev - m_new)
        kw = kh * jnp.exp(g_col - m_new)
        c_ref[h] = decay * cmat + lax.dot_general(
            kw.astype(BF16), vh, (((0,), (0,)), ((), ())), preferred_element_type=F32)
        n_ref[h:h + 1, :] = decay * n_row + jnp.sum(kw, axis=0, keepdims=True)
        m_ref[h:h + 1, :] = jnp.broadcast_to(m_new, (1, LANES_V7X))

        ms = jnp.mean(hh * hh, axis=-1, keepdims=True)
        hn = hh * lax.rsqrt(ms + EPS) * hn_ref[:, cs]
        out_ref[:, cs] = (jax.nn.sigmoid(o_ref[:, cs].astype(F32)) * hn).astype(BF16)


def _mlstm(proj, gates, gbias, conv_w, conv_b, headnorm, batch, seq, length):
    t_rows = proj.shape[0]
    nc = seq // length
    col = lambda k: pl.BlockSpec((length, BRANCH_W), lambda b, c: (b * nc + c, k))
    const = lambda shape: pl.BlockSpec(shape, lambda b, c: (0, 0))
    return pl.pallas_call(
        functools.partial(_mlstm_kernel, length=length),
        grid=(batch, nc),
        in_specs=[
            col(3), col(4), col(5), col(6),
            pl.BlockSpec((length, LANES_V7X), lambda b, c: (b * nc + c, 0)),
            const((1, LANES_V7X)),
            const((CONV_W, 2 * BRANCH_W)),
            const((1, 2 * BRANCH_W)),
            const((1, BRANCH_W)),
        ],
        out_specs=pl.BlockSpec((length, BRANCH_W), lambda b, c: (b * nc + c, 0)),
        out_shape=jax.ShapeDtypeStruct((t_rows, BRANCH_W), BF16),
        scratch_shapes=[
            pltpu.VMEM((ML_HEADS, HEAD_W, HEAD_W), F32),
            pltpu.VMEM((ML_HEADS, HEAD_W), F32),
            pltpu.VMEM((ML_HEADS, LANES_V7X), F32),
            pltpu.VMEM((SUBLANES_V7X + length, BRANCH_W), F32),
            pltpu.VMEM((SUBLANES_V7X + length, BRANCH_W), F32),
        ],
        compiler_params=_cparams(("parallel", "arbitrary")),
        name="mlstm",
    )(proj, proj, proj, proj, gates, gbias, conv_w, conv_b, headnorm)


def _merge_kernel(a_ref, b_ref, wa_ref, wb_ref, ga_ref, gb_ref, out_ref):
    pa = jnp.dot(a_ref[...], wa_ref[...], preferred_element_type=F32)
    pb = jnp.dot(b_ref[...], wb_ref[...], preferred_element_type=F32)
    out_ref[...] = (ga_ref[...].astype(F32) * pa + gb_ref[...].astype(F32) * pb).astype(BF16)


def _merge(branch_a, branch_b, wa, wb, proj, d_model, tm, tn):
    t_rows = branch_a.shape[0]
    ga0 = N_PLAIN_COLS // tn
    gb0 = (N_PLAIN_COLS + d_model) // tn
    return pl.pallas_call(
        _merge_kernel,
        grid=(t_rows // tm, d_model // tn),
        in_specs=[
            pl.BlockSpec((tm, BRANCH_W), lambda i, j: (i, 0)),
            pl.BlockSpec((tm, BRANCH_W), lambda i, j: (i, 0)),
            pl.BlockSpec((BRANCH_W, tn), lambda i, j: (0, j)),
            pl.BlockSpec((BRANCH_W, tn), lambda i, j: (0, j)),
            pl.BlockSpec((tm, tn), lambda i, j: (i, ga0 + j)),
            pl.BlockSpec((tm, tn), lambda i, j: (i, gb0 + j)),
        ],
        out_specs=pl.BlockSpec((tm, tn), lambda i, j: (i, j)),
        out_shape=jax.ShapeDtypeStruct((t_rows, d_model), BF16),
        compiler_params=_cparams(("parallel", "arbitrary")),
        name="merge",
    )(branch_a, branch_b, wa, wb, proj, proj)


def _outproj_kernel(x_ref, mg_ref, w_ref, g_ref, x1t_ref, xnt_ref):
    x1 = x_ref[...] + jnp.dot(mg_ref[...], w_ref[...], preferred_element_type=F32)
    ms = jnp.mean(x1 * x1, axis=-1, keepdims=True)
    xn = x1 * lax.rsqrt(ms + EPS) * g_ref[...]
    x1t_ref[...] = x1.T
    xnt_ref[...] = xn.T.astype(BF16)


def _outproj(x2, merged, w_out, g, tm):
    t_rows, d = x2.shape
    return pl.pallas_call(
        _outproj_kernel,
        grid=(t_rows // tm,),
        in_specs=[
            pl.BlockSpec((tm, d), lambda i: (i, 0)),
            pl.BlockSpec((tm, d), lambda i: (i, 0)),
            pl.BlockSpec((d, d), lambda i: (0, 0)),
            pl.BlockSpec((1, d), lambda i: (0, 0)),
        ],
        out_specs=[
            pl.BlockSpec((d, tm), lambda i: (0, i)),
            pl.BlockSpec((d, tm), lambda i: (0, i)),
        ],
        out_shape=[
            jax.ShapeDtypeStruct((d, t_rows), F32),
            jax.ShapeDtypeStruct((d, t_rows), BF16),
        ],
        compiler_params=_cparams(("parallel",)),
        name="outproj",
    )(x2, merged, w_out, g)


def _top16(s):
    nk, tm = s.shape
    idx = lax.broadcasted_iota(jnp.int32, (nk, tm), 0)
    rank = jnp.zeros((nk, tm), F32)
    cur = s
    vals = []
    for r in range(TOPK):
        mx = jnp.max(cur, axis=0, keepdims=True)
        first = jnp.min(jnp.where(cur == mx, idx, nk), axis=0, keepdims=True)
        sel = idx == first
        rank = jnp.where(sel, float(r + 1), rank)
        cur = jnp.where(sel, NEG_INF, cur)
        vals.append(mx)
    return rank, jnp.concatenate(vals, axis=0)


def _staircase(a, b):
    tm = a.shape[1]
    ridx = lax.broadcasted_iota(jnp.int32, (TOPK, tm), 0)
    top = a[0:1] + b[0:1]
    cnt = jnp.zeros((TOPK, tm), jnp.int32)
    z = jnp.zeros((1, tm), F32)
    for _ in range(TOPK):
        nxt = jnp.full((TOPK, tm), NEG_INF, F32)
        for c in range(TOPK):
            nxt = jnp.where(cnt == c, b[c:c + 1], nxt)
        f = a + nxt
        mx = jnp.max(f, axis=0, keepdims=True)
        first = jnp.min(jnp.where(f == mx, ridx, TOPK), axis=0, keepdims=True)
        cnt = cnt + (ridx == first).astype(jnp.int32)
        z = z + jnp.exp(mx - top)
    return cnt.astype(F32), z


def _peer_rank_kernel(xnt_ref, wq_ref, sk_ref, e0_ref, n0_ref, e1_ref, r1_ref, qt_ref):
    qt_ref[...] = jnp.dot(wq_ref[...], xnt_ref[...], preferred_element_type=F32).astype(BF16)
    for h in range(PEER_HEADS):
        base = h * 2 * N_KEYS
        s0 = jnp.dot(sk_ref[h, 0], qt_ref[base:base + N_KEYS, :], preferred_element_type=F32)
        s1 = jnp.dot(sk_ref[h, 1], qt_ref[base + N_KEYS:base + 2 * N_KEYS, :], preferred_element_type=F32)
        r0, a = _top16(s0)
        r1, b = _top16(s1)
        cnt, z = _staircase(a, b)
        n0 = jnp.zeros_like(r0)
        for r in range(TOPK):
            n0 = jnp.where(r0 == float(r + 1), cnt[r:r + 1], n0)
        e0_ref[h] = jnp.exp(s0 - a[0:1]) / z
        n0_ref[h] = n0
        e1_ref[h] = jnp.exp(s1 - b[0:1]).astype(BF16)
        r1_ref[h] = jnp.where(r1 == 0.0, float(2 * TOPK), r1).astype(BF16)


def _peer_rank(xnt, wq_t, subkeys, tm):
    d, t_rows = xnt.shape
    qd = wq_t.shape[0]
    head_f32 = jax.ShapeDtypeStruct((PEER_HEADS, N_KEYS, t_rows), F32)
    head_b16 = jax.ShapeDtypeStruct((PEER_HEADS, N_KEYS, t_rows), BF16)
    head_spec = pl.BlockSpec((PEER_HEADS, N_KEYS, tm), lambda i: (0, 0, i))
    return pl.pallas_call(
        _peer_rank_kernel,
        grid=(t_rows // tm,),
        in_specs=[
            pl.BlockSpec((d, tm), lambda i: (0, i)),
            pl.BlockSpec((qd, d), lambda i: (0, 0)),
            pl.BlockSpec((PEER_HEADS, 2, N_KEYS, N_KEYS), lambda i: (0, 0, 0, 0)),
        ],
        out_specs=[head_spec] * 4,
        out_shape=[head_f32, head_f32, head_b16, head_b16],
        scratch_shapes=[pltpu.VMEM((qd, tm), BF16)],
        compiler_params=_cparams(("parallel",)),
        name="peer_rank",
    )(xnt, wq_t, subkeys)


def _peer_mix_kernel(xnt_ref, x1t_ref, u_ref, vt_ref, e0_ref, n0_ref, e1_ref, r1_ref, g_ref,
                     y_ref, acc_ref, a0_ref, a1_ref, *, eb, n_blocks):
    si = pl.program_id(1)
    n_i = eb // N_KEYS
    tm = acc_ref.shape[1]

    @pl.when(si == 0)
    def _():
        acc_ref[...] = jnp.zeros_like(acc_ref)
        a1_ref[...] = jnp.zeros_like(a1_ref)

    def step(cur_ref, prev_ref):
        blk = jnp.maximum(si - 1, 0)
        tiles = []
        for ii in range(n_i):
            i = blk * n_i + ii
            shape = (N_KEYS, tm)
            wgt = jnp.zeros(shape, BF16)
            for h in range(PEER_HEADS):
                n_tile = jnp.broadcast_to(n0_ref[h, pl.ds(i, 1), :], shape).astype(BF16)
                e_tile = jnp.broadcast_to(e0_ref[h, pl.ds(i, 1), :], shape).astype(BF16)
                wgt = wgt + e_tile * jnp.where(r1_ref[h] <= n_tile, e1_ref[h], jnp.zeros(shape, BF16))
            rows = slice(ii * N_KEYS, (ii + 1) * N_KEYS)
            tiles.append(jax.nn.gelu(prev_ref[rows, :]).astype(BF16) * wgt)
        act = jnp.concatenate(tiles, axis=0)
        cur_ref[...] = jnp.dot(u_ref[...], xnt_ref[...], preferred_element_type=F32)
        acc_ref[...] += jnp.dot(vt_ref[...], act, preferred_element_type=F32)

    parity = lax.rem(si, 2)

    @pl.when(parity == 0)
    def _():
        step(a0_ref, a1_ref)

    @pl.when(parity == 1)
    def _():
        step(a1_ref, a0_ref)

    @pl.when(si == n_blocks)
    def _():
        x2 = (x1t_ref[...] + acc_ref[...]).T
        ms = jnp.mean(x2 * x2, axis=-1, keepdims=True)
        y_ref[...] = x2 * lax.rsqrt(ms + EPS) * g_ref[...]


def _peer_mix(xnt, x1t, u, vt, e0, n0, e1, r1, g, tm, eb):
    d, t_rows = xnt.shape
    nb = u.shape[0] // eb
    head_spec = pl.BlockSpec((PEER_HEADS, N_KEYS, tm), lambda t, s: (0, 0, t))
    return pl.pallas_call(
        functools.partial(_peer_mix_kernel, eb=eb, n_blocks=nb),
        grid=(t_rows // tm, nb + 1),
        in_specs=[
            pl.BlockSpec((d, tm), lambda t, s: (0, t)),
            pl.BlockSpec((d, tm), lambda t, s: (0, t)),
            pl.BlockSpec((eb, d), lambda t, s: (jnp.minimum(s, nb - 1), 0)),
            pl.BlockSpec((d, eb), lambda t, s: (0, jnp.maximum(s - 1, 0))),
            head_spec, head_spec, head_spec, head_spec,
            pl.BlockSpec((1, d), lambda t, s: (0, 0)),
        ],
        out_specs=pl.BlockSpec((tm, d), lambda t, s: (t, 0)),
        out_shape=jax.ShapeDtypeStruct((t_rows, d), F32),
        scratch_shapes=[pltpu.VMEM((d, tm), F32), pltpu.VMEM((eb, tm), F32), pltpu.VMEM((eb, tm), F32)],
        compiler_params=_cparams(("parallel", "arbitrary")),
        name="peer_mix",
    )(xnt, x1t, u, vt, e0, n0, e1, r1, g)


def kernel(x, norm_mix, w_in, conv_w, conv_b, ml_gate_bias, da_lambda, da_subln, ml_headnorm,
           w_branch_a, w_branch_b, w_out, norm_ffn, peer_query, peer_subkeys, expert_u, expert_v,
           norm_final):
    batch, seq, d = x.shape
    t_rows = batch * seq
    assert norm_mix.shape[0] == 1, "single-layer trunk"
    assert seq % CHUNK == 0 and d % LANES_V7X == 0
    ts = _tiles(t_rows, seq, d)
    x2 = x.reshape(t_rows, d)

    w = w_in[0]
    gate0 = N_PLAIN_COLS + N_GATE_LOGITS
    w_cat = jnp.concatenate([w[:, :N_PLAIN_COLS], w[:, gate0:]], axis=1).astype(BF16)
    w_if = jnp.pad(w[:, N_PLAIN_COLS:gate0], ((0, 0), (0, LANES_V7X - N_GATE_LOGITS))).astype(BF16)
    gbias = jnp.pad(ml_gate_bias[0], (0, LANES_V7X - N_GATE_LOGITS)).reshape(1, LANES_V7X)
    slopes = 2.0 ** (-8.0 * jnp.arange(1, DA_HEADS + 1, dtype=F32) / DA_HEADS)
    slopes = jnp.broadcast_to(slopes[:, None, None], (DA_HEADS, 1, LANES_V7X))

    proj, gates = _inproj(x2, norm_mix[0].reshape(1, d), w_cat, w_if, ts["proj_tm"], ts["proj_tn"])
    branch_a = _attention(proj, slopes, da_lambda[0], da_subln[0].reshape(HEAD_W, 1), batch, seq, ts["attn_tq"])
    branch_b = _mlstm(proj, gates, gbias, conv_w[0], conv_b[0].reshape(1, -1),
                      ml_headnorm[0].reshape(1, BRANCH_W), batch, seq, ts["ml_len"])
    merged = _merge(branch_a, branch_b, w_branch_a[0].astype(BF16), w_branch_b[0].astype(BF16),
                    proj, d, ts["merge_tm"], ts["merge_tn"])
    x1t, xnt = _outproj(x2, merged, w_out[0].astype(BF16), norm_ffn[0].reshape(1, d), ts["out_tm"])
    e0, n0, e1, r1 = _peer_rank(xnt, peer_query[0].T.astype(BF16), peer_subkeys[0].astype(BF16), ts["peer_tm"])
    y = _peer_mix(xnt, x1t, expert_u[0].astype(BF16), expert_v[0].T.astype(BF16), e0, n0, e1, r1,
                  norm_final.reshape(1, d), ts["peer_tm"], ts["peer_eb"])
    return y.reshape(batch, seq, d)
```

```python
import functools

import jax
import jax.numpy as jnp
from jax import lax
from jax.experimental import pallas as pl
from jax.experimental.pallas import tpu as pltpu

F32 = jnp.float32
BF16 = jnp.bfloat16

CHUNK = 64
EPS = 1e-6
LAMBDA_INIT = 0.8 - 0.6
DA_HEADS = 8
DA_QK = 64
HEAD_W = 128
BRANCH_W = DA_HEADS * HEAD_W
ML_HEADS = 8
CONV_W = 4
PEER_HEADS = 8
N_KEYS = 128
TOPK = 16
N_PLAIN_COLS = 7 * BRANCH_W
N_GATE_LOGITS = 2 * ML_HEADS

LANES_V7X = 128
SUBLANES_V7X = 8
VMEM_LIMIT_V7X = 56 * 1024 * 1024

NEG_INF = float("-inf")


def _cparams(sem):
    return pltpu.CompilerParams(dimension_semantics=sem, vmem_limit_bytes=VMEM_LIMIT_V7X)


def _tiles(t_rows, seq, d_model):
    return dict(
        proj_tm=min(1024, t_rows), proj_tn=1024 if d_model % 1024 == 0 else 512,
        attn_tq=min(512, seq),
        ml_len=min(256, seq),
        merge_tm=min(512, t_rows), merge_tn=min(512, d_model),
        out_tm=min(256, t_rows),
        peer_tm=min(512, t_rows), peer_eb=512,
    )


def _inproj_kernel(x_ref, g_ref, w_ref, wif_ref, proj_ref, if_ref, h_ref, *, n_plain):
    j = pl.program_id(1)

    @pl.when(j == 0)
    def _():
        x = x_ref[...]
        ms = jnp.mean(x * x, axis=-1, keepdims=True)
        h_ref[...] = (x * lax.rsqrt(ms + EPS) * g_ref[...]).astype(BF16)
        if_ref[...] = jnp.dot(h_ref[...], wif_ref[...], preferred_element_type=F32)

    acc = jnp.dot(h_ref[...], w_ref[...], preferred_element_type=F32)

    @pl.when(j < n_plain)
    def _():
        proj_ref[...] = acc.astype(BF16)

    @pl.when(j >= n_plain)
    def _():
        proj_ref[...] = jax.nn.sigmoid(acc).astype(BF16)


def _inproj(x2, g, w_cat, w_if, tm, tn):
    t_rows, d = x2.shape
    n = w_cat.shape[1]
    return pl.pallas_call(
        functools.partial(_inproj_kernel, n_plain=N_PLAIN_COLS // tn),
        grid=(t_rows // tm, n // tn),
        in_specs=[
            pl.BlockSpec((tm, d), lambda i, j: (i, 0)),
            pl.BlockSpec((1, d), lambda i, j: (0, 0)),
            pl.BlockSpec((d, tn), lambda i, j: (0, j)),
            pl.BlockSpec((d, LANES_V7X), lambda i, j: (0, 0)),
        ],
        out_specs=[
            pl.BlockSpec((tm, tn), lambda i, j: (i, j)),
            pl.BlockSpec((tm, LANES_V7X), lambda i, j: (i, 0)),
        ],
        out_shape=[
            jax.ShapeDtypeStruct((t_rows, n), BF16),
            jax.ShapeDtypeStruct((t_rows, LANES_V7X), F32),
        ],
        scratch_shapes=[pltpu.VMEM((tm, d), BF16)],
        compiler_params=_cparams(("parallel", "arbitrary")),
        name="inproj",
    )(x2, g, w_cat, w_if)


def _attn_kernel(slope_ref, lam_ref, subg_ref, q_ref, k_ref, v_ref, o_ref,
                 kpos_ref, qpos_ref, bdiag_ref, ta_ref, tb_ref, m_ref, l_ref, acc_ref, *, tq):
    qi = pl.program_id(2)
    slope = slope_ref[0:1, 0:1]
    cols = 2 * tq
    split = 16

    @pl.when(qi == 0)
    def _():
        c = lax.broadcasted_iota(jnp.int32, (tq, cols), 0)
        r = lax.broadcasted_iota(jnp.int32, (tq, cols), 1)
        r = jnp.where(r >= tq, r - tq, r)
        allowed = (c // CHUNK) <= (r // CHUNK)
        bdiag_ref[...] = jnp.where(allowed, -slope * jnp.abs(r - c).astype(F32), NEG_INF)
        lane = lax.broadcasted_iota(jnp.int32, (tq, HEAD_W), 1)
        off = lax.broadcasted_iota(jnp.int32, (tq, HEAD_W), 0)
        hi = (slope * split) * (off // split).astype(F32)
        lo = slope * (off % split).astype(F32)
        one = jnp.ones((tq, HEAD_W), F32)
        zero = jnp.zeros((tq, HEAD_W), F32)
        kpos_ref[...] = jnp.where(lane == 0, hi, jnp.where(lane == 1, lo, jnp.where(lane < 4, one, zero))).astype(BF16)
        qp = jnp.where(lane == 2, -hi, jnp.where(lane == 3, -lo, jnp.where(lane < 2, one, zero))).astype(BF16)
        qpos_ref[0:tq, :] = qp
        qpos_ref[tq:cols, :] = qp

    q = q_ref[...] * jnp.asarray(DA_QK ** -0.5, BF16)
    lane = lax.broadcasted_iota(jnp.int32, (tq, HEAD_W), 1)
    zero = jnp.zeros_like(q)
    q2 = jnp.concatenate([jnp.where(lane < DA_QK, q, zero), jnp.where(lane >= DA_QK, q, zero)], axis=0)
    q2_pos = jnp.concatenate([q2, qpos_ref[...]], axis=1)

    m_ref[...] = jnp.full((1, cols), NEG_INF, F32)
    l_ref[...] = jnp.zeros((1, cols), F32)
    acc_ref[...] = jnp.zeros((HEAD_W, cols), F32)
    nt = (((1,), (1,)), ((), ()))

    def softmax_step(t, v, shift_const):
        m_old = m_ref[...]
        m_new = jnp.maximum(m_old, jnp.max(t, axis=0, keepdims=True) + shift_const)
        p = jnp.exp(t - (m_new - shift_const))
        alpha = jnp.exp(m_old - m_new)
        l_ref[...] = alpha * l_ref[...] + jnp.sum(p, axis=0, keepdims=True)
        pv = lax.dot_general(v, p.astype(BF16), (((0,), (0,)), ((), ())), preferred_element_type=F32)
        acc_ref[...] = alpha * acc_ref[...] + pv
        m_ref[...] = m_new

    last = jnp.maximum(qi - 1, 0)

    def scores(blk, t_ref):
        k0 = pl.multiple_of(jnp.minimum(blk, last) * tq, tq)
        k_pos = jnp.concatenate([k_ref[pl.ds(k0, tq), :], kpos_ref[...]], axis=1)
        t_ref[...] = lax.dot_general(k_pos, q2_pos, nt, preferred_element_type=F32)

    def consume(blk, t_ref):
        kb = jnp.minimum(blk, last)
        k0 = pl.multiple_of(kb * tq, tq)
        shift = jnp.where(blk < qi, -slope * ((qi - kb) * tq).astype(F32), NEG_INF)
        softmax_step(t_ref[...], v_ref[pl.ds(k0, tq), :], shift)

    scores(0, ta_ref)
    kd = pl.multiple_of(qi * tq, tq)
    s = lax.dot_general(k_ref[pl.ds(kd, tq), :], q2, nt, preferred_element_type=F32)
    softmax_step(s + bdiag_ref[...], v_ref[pl.ds(kd, tq), :], jnp.zeros((1, 1), F32))

    def body(i, carry):
        scores(2 * i + 1, tb_ref)
        consume(2 * i, ta_ref)
        scores(2 * i + 2, ta_ref)
        consume(2 * i + 1, tb_ref)
        return carry

    lax.fori_loop(0, (qi + 1) // 2, body, 0)

    lq = lam_ref[...]
    lam = (jnp.exp(jnp.sum(lq[0:1] * lq[1:2], keepdims=True))
           - jnp.exp(jnp.sum(lq[2:3] * lq[3:4], keepdims=True)) + LAMBDA_INIT)
    o = acc_ref[...] / l_ref[...]
    a = o[:, :tq] - lam * o[:, tq:]
    ms = jnp.mean(a * a, axis=0, keepdims=True)
    y = a * lax.rsqrt(ms + EPS) * subg_ref[...]
    o_ref[...] = (y * (1.0 - LAMBDA_INIT)).T.astype(BF16)


def _attention(proj, slopes, da_lambda, subg, batch, seq, tq):
    t_rows = proj.shape[0]
    nq = seq // tq
    return pl.pallas_call(
        functools.partial(_attn_kernel, tq=tq),
        grid=(batch, DA_HEADS, nq),
        in_specs=[
            pl.BlockSpec((None, 1, LANES_V7X), lambda b, h, i: (h, 0, 0)),
            pl.BlockSpec((4, DA_QK), lambda b, h, i: (0, 0)),
            pl.BlockSpec((HEAD_W, 1), lambda b, h, i: (0, 0)),
            pl.BlockSpec((tq, HEAD_W), lambda b, h, i: (b * nq + i, h)),
            pl.BlockSpec((seq, HEAD_W), lambda b, h, i: (b, DA_HEADS + h)),
            pl.BlockSpec((seq, HEAD_W), lambda b, h, i: (b, 2 * DA_HEADS + h)),
        ],
        out_specs=pl.BlockSpec((tq, HEAD_W), lambda b, h, i: (b * nq + i, h)),
        out_shape=jax.ShapeDtypeStruct((t_rows, BRANCH_W), BF16),
        scratch_shapes=[
            pltpu.VMEM((tq, HEAD_W), BF16),
            pltpu.VMEM((2 * tq, HEAD_W), BF16),
            pltpu.VMEM((tq, 2 * tq), F32),
            pltpu.VMEM((tq, 2 * tq), F32),
            pltpu.VMEM((tq, 2 * tq), F32),
            pltpu.VMEM((1, 2 * tq), F32),
            pltpu.VMEM((1, 2 * tq), F32),
            pltpu.VMEM((HEAD_W, 2 * tq), F32),
        ],
        compiler_params=_cparams(("parallel", "parallel", "arbitrary")),
        name="diff_attention",
    )(slopes, da_lambda, subg, proj, proj, proj)


def _mlstm_kernel(q_ref, k_ref, v_ref, o_ref, gate_ref, gbias_ref, cw_ref, cb_ref, hn_ref,
                  out_ref, c_ref, n_ref, m_ref, qext_ref, kext_ref, *, length):
    ci = pl.program_id(1)
    halo = SUBLANES_V7X

    @pl.when(ci == 0)
    def _():
        c_ref[...] = jnp.zeros_like(c_ref)
        n_ref[...] = jnp.zeros_like(n_ref)
        m_ref[...] = jnp.zeros_like(m_ref)
        qext_ref[0:halo, :] = jnp.zeros((halo, BRANCH_W), F32)
        kext_ref[0:halo, :] = jnp.zeros((halo, BRANCH_W), F32)

    def conv_silu(x_ref, ext_ref, col0):
        ext_ref[halo:, :] = x_ref[...].astype(F32)
        y = cb_ref[:, col0:col0 + BRANCH_W]
        for j in range(CONV_W):
            y = y + cw_ref[j:j + 1, col0:col0 + BRANCH_W] * ext_ref[pl.ds(halo - (CONV_W - 1) + j, length), :]
        ext_ref[0:halo, :] = ext_ref[length:length + halo, :]
        return y * jax.nn.sigmoid(y)

    qc = conv_silu(q_ref, qext_ref, 0).astype(BF16)
    kc = conv_silu(k_ref, kext_ref, BRANCH_W) * (HEAD_W ** -0.5)
    kc_b = kc.astype(BF16)

    g = gate_ref[...] + gbias_ref[...]
    lane = lax.broadcasted_iota(jnp.int32, (length, LANES_V7X), 1)
    lg = jnp.where(lane < ML_HEADS, g, jax.nn.log_sigmoid(g))
    lgt = lg.T
    r = lax.broadcasted_iota(jnp.int32, (length, length), 0)
    c = lax.broadcasted_iota(jnp.int32, (length, length), 1)
    causal = r >= c
    tril = causal.astype(F32)
    triu = (r <= c).astype(F32)
    hi = lax.Precision.HIGHEST
    bcol_all = jnp.dot(tril, lg, precision=hi, preferred_element_type=F32)
    brow_all = jnp.dot(lgt[ML_HEADS:2 * ML_HEADS], triu, precision=hi, preferred_element_type=F32)
    lirow_all = lgt[0:ML_HEADS]

    for h in range(ML_HEADS):
        cs = slice(h * HEAD_W, (h + 1) * HEAD_W)
        b_col = bcol_all[:, ML_HEADS + h:ML_HEADS + h + 1]
        li_col = lg[:, h:h + 1]
        b_row = brow_all[h:h + 1, :]
        li_row = lirow_all[h:h + 1, :]
        m_prev = m_ref[h:h + 1, 0:1]
        n_row = n_ref[h:h + 1, :]
        cmat = c_ref[h]
        qh = qc[:, cs]
        kh = kc[:, cs]
        vh = v_ref[:, cs]

        d = jnp.where(causal, b_col - b_row + li_row, NEG_INF)
        inter = b_col + m_prev
        m_t = jnp.maximum(jnp.max(d, axis=-1, keepdims=True), inter)
        w = jnp.exp(d - m_t)
        sc = jnp.exp(inter - m_t)
        s = lax.dot_general(qh, kc_b[:, cs], (((1,), (1,)), ((), ())), preferred_element_type=F32)
        wqk = w * s
        num = (sc * jnp.dot(qh, cmat.astype(BF16), preferred_element_type=F32)
               + jnp.dot(wqk.astype(BF16), vh, preferred_element_type=F32))
        den = (sc * jnp.sum(qh.astype(F32) * n_row, axis=-1, keepdims=True)
               + jnp.sum(wqk, axis=-1, keepdims=True))
        den = jnp.maximum(jnp.abs(den), jnp.exp(-m_t))
        hh = num / den

        b_last = b_col[length - 1:length, :]
        g_col = b_last - b_col + li_col
        m_new = jnp.maximum(b_last + m_prev, jnp.max(g_col, axis=0, keepdims=True))
        decay = jnp.exp(b_last + m_prev - m_new)
        kw = kh * jnp.exp(g_col - m_new)
        c_ref[h] = decay * cmat + lax.dot_general(
            kw.astype(BF16), vh, (((0,), (0,)), ((), ())), preferred_element_type=F32)
        n_ref[h:h + 1, :] = decay * n_row + jnp.sum(kw, axis=0, keepdims=True)
        m_ref[h:h + 1, :] = jnp.broadcast_to(m_new, (1, LANES_V7X))

        ms = jnp.mean(hh * hh, axis=-1, keepdims=True)
        hn = hh * lax.rsqrt(ms + EPS) * hn_ref[:, cs]
        out_ref[:, cs] = (jax.nn.sigmoid(o_ref[:, cs].astype(F32)) * hn).astype(BF16)


def _mlstm(proj, gates, gbias, conv_w, conv_b, headnorm, batch, seq, length):
    t_rows = proj.shape[0]
    nc = seq // length
    col = lambda k: pl.BlockSpec((length, BRANCH_W), lambda b, c: (b * nc + c, k))
    const = lambda shape: pl.BlockSpec(shape, lambda b, c: (0, 0))
    return pl.pallas_call(
        functools.partial(_mlstm_kernel, length=length),
        grid=(batch, nc),
        in_specs=[
            col(3), col(4), col(5), col(6),
            pl.BlockSpec((length, LANES_V7X), lambda b, c: (b * nc + c, 0)),
            const((1, LANES_V7X)),
            const((CONV_W, 2 * BRANCH_W)),
            const((1, 2 * BRANCH_W)),
            const((1, BRANCH_W)),
        ],
        out_specs=pl.BlockSpec((length, BRANCH_W), lambda b, c: (b * nc + c, 0)),
        out_shape=jax.ShapeDtypeStruct((t_rows, BRANCH_W), BF16),
        scratch_shapes=[
            pltpu.VMEM((ML_HEADS, HEAD_W, HEAD_W), F32),
            pltpu.VMEM((ML_HEADS, HEAD_W), F32),
            pltpu.VMEM((ML_HEADS, LANES_V7X), F32),
            pltpu.VMEM((SUBLANES_V7X + length, BRANCH_W), F32),
            pltpu.VMEM((SUBLANES_V7X + length, BRANCH_W), F32),
        ],
        compiler_params=_cparams(("parallel", "arbitrary")),
        name="mlstm",
    )(proj, proj, proj, proj, gates, gbias, conv_w, conv_b, headnorm)


def _merge_kernel(a_ref, b_ref, wa_ref, wb_ref, ga_ref, gb_ref, out_ref):
    pa = jnp.dot(a_ref[...], wa_ref[...], preferred_element_type=F32)
    pb = jnp.dot(b_ref[...], wb_ref[...], preferred_element_type=F32)
    out_ref[...] = (ga_ref[...].astype(F32) * pa + gb_ref[...].astype(F32) * pb).astype(BF16)


def _merge(branch_a, branch_b, wa, wb, proj, d_model, tm, tn):
    t_rows = branch_a.shape[0]
    ga0 = N_PLAIN_COLS // tn
    gb0 = (N_PLAIN_COLS + d_model) // tn
    return pl.pallas_call(
        _merge_kernel,
        grid=(t_rows // tm, d_model // tn),
        in_specs=[
            pl.BlockSpec((tm, BRANCH_W), lambda i, j: (i, 0)),
            pl.BlockSpec((tm, BRANCH_W), lambda i, j: (i, 0)),
            pl.BlockSpec((BRANCH_W, tn), lambda i, j: (0, j)),
            pl.BlockSpec((BRANCH_W, tn), lambda i, j: (0, j)),
            pl.BlockSpec((tm, tn), lambda i, j: (i, ga0 + j)),
            pl.BlockSpec((tm, tn), lambda i, j: (i, gb0 + j)),
        ],
        out_specs=pl.BlockSpec((tm, tn), lambda i, j: (i, j)),
        out_shape=jax.ShapeDtypeStruct((t_rows, d_model), BF16),
        compiler_params=_cparams(("parallel", "arbitrary")),
        name="merge",
    )(branch_a, branch_b, wa, wb, proj, proj)


def _outproj_kernel(x_ref, mg_ref, w_ref, g_ref, x1t_ref, xnt_ref):
    x1 = x_ref[...] + jnp.dot(mg_ref[...], w_ref[...], preferred_element_type=F32)
    ms = jnp.mean(x1 * x1, axis=-1, keepdims=True)
    xn = x1 * lax.rsqrt(ms + EPS) * g_ref[...]
    x1t_ref[...] = x1.T
    xnt_ref[...] = xn.T.astype(BF16)


def _outproj(x2, merged, w_out, g, tm):
    t_rows, d = x2.shape
    return pl.pallas_call(
        _outproj_kernel,
        grid=(t_rows // tm,),
        in_specs=[
            pl.BlockSpec((tm, d), lambda i: (i, 0)),
            pl.BlockSpec((tm, d), lambda i: (i, 0)),
            pl.BlockSpec((d, d), lambda i: (0, 0)),
            pl.BlockSpec((1, d), lambda i: (0, 0)),
        ],
        out_specs=[
            pl.BlockSpec((d, tm), lambda i: (0, i)),
            pl.BlockSpec((d, tm), lambda i: (0, i)),
        ],
        out_shape=[
            jax.ShapeDtypeStruct((d, t_rows), F32),
            jax.ShapeDtypeStruct((d, t_rows), BF16),
        ],
        compiler_params=_cparams(("parallel",)),
        name="outproj",
    )(x2, merged, w_out, g)


INT_MIN = -2 ** 31


def _sortable(x):
    bits = lax.bitcast_convert_type(x, jnp.int32)
    return bits ^ ((bits >> 31) & 0x7FFFFFFF)


def _unsortable(k):
    return lax.bitcast_convert_type(k ^ ((k >> 31) & 0x7FFFFFFF), F32)


def _top16_ranks(s_ref, shape3, key_ref, val_ref):
    def load_keys():
        key_ref[...] = _sortable(s_ref[...].reshape(shape3) + 0.0)

    def run(extract):
        load_keys()
        lax.fori_loop(0, TOPK, extract, jnp.max(key_ref[...], axis=0))

    def take_all(r, mx):
        val_ref[r] = mx
        cur = key_ref[...]
        cur = jnp.where(cur == mx[None], INT_MIN + 1 + r, cur)
        key_ref[...] = cur
        return jnp.max(cur, axis=0)

    def take_first(r, mx):
        val_ref[r] = mx
        cur = key_ref[...]
        kidx = lax.broadcasted_iota(jnp.int32, shape3, 0)
        first = jnp.min(jnp.where(cur == mx[None], kidx, N_KEYS), axis=0)
        cur = jnp.where(kidx == first[None], INT_MIN + 1 + r, cur)
        key_ref[...] = cur
        return jnp.max(cur, axis=0)

    run(take_all)
    taken = jnp.sum((key_ref[...] <= INT_MIN + TOPK).astype(F32), axis=0)

    @pl.when(jnp.max(taken) > float(TOPK))
    def _():
        run(take_first)

    cur = key_ref[...]
    return jnp.where(cur <= INT_MIN + TOPK, cur - INT_MIN, 0)


def _staircase(a, b):
    ridx = lax.broadcasted_iota(jnp.int32, a.shape, 0)
    top = a[0] + b[0]
    cnt = jnp.zeros(a.shape, jnp.int32)
    nxt = jnp.broadcast_to(b[0][None], a.shape)
    z = jnp.zeros(top.shape, F32)
    for _ in range(TOPK):
        f = a + nxt
        mx = jnp.max(f, axis=0)
        first = jnp.min(jnp.where(f == mx[None], ridx, TOPK), axis=0)
        sel = ridx == first[None]
        cnt = jnp.where(sel, cnt + 1, cnt)
        taken = jnp.max(jnp.where(sel, cnt, 0), axis=0)
        b_next = jnp.full(top.shape, NEG_INF, F32)
        for c in range(1, TOPK):
            b_next = jnp.where(taken == c, b[c], b_next)
        nxt = jnp.where(sel, b_next[None], nxt)
        z = z + jnp.exp(mx - top)
    return cnt, z


def _peer_rank_kernel(xnt_ref, wq_ref, skb_ref, perm_ref, e0_ref, n0_ref, e1_ref, r1_ref,
                      qt_ref, s_ref, key_ref, rank0_ref, a_ref, b_ref):
    tm = xnt_ref.shape[1]
    rows = N_KEYS * PEER_HEADS
    shape3 = (N_KEYS, PEER_HEADS, tm)
    qt_ref[...] = jnp.dot(wq_ref[...], xnt_ref[...], preferred_element_type=F32).astype(BF16)
    for c in range(2):
        s_ref[c] = jnp.dot(skb_ref[c], qt_ref[c * rows:(c + 1) * rows, :], preferred_element_type=F32)

    rank0_ref[...] = _top16_ranks(s_ref.at[0], shape3, key_ref, a_ref)
    rank1 = _top16_ranks(s_ref.at[1], shape3, key_ref, b_ref)
    a = _unsortable(a_ref[...])
    b = _unsortable(b_ref[...])
    cnt, z = _staircase(a, b)

    rank0 = rank0_ref[...]
    n0 = jnp.zeros(shape3, jnp.int32)
    for r in range(TOPK):
        n0 = jnp.where(rank0 == r + 1, cnt[r][None], n0)
    n0_ref[...] = n0.astype(F32)
    e0_ref[...] = jnp.exp(s_ref[0].reshape(shape3) - a[0][None]) * (1.0 / z)[None]

    r1 = jnp.where(rank1 == 0, 2 * TOPK, rank1).astype(F32).astype(BF16).reshape(rows, tm)
    e1 = jnp.exp(s_ref[1].reshape(shape3) - b[0][None]).astype(BF16).reshape(rows, tm)
    out3 = (PEER_HEADS, N_KEYS, tm)
    r1_ref[...] = jnp.dot(perm_ref[...], r1, preferred_element_type=F32).astype(BF16).reshape(out3)
    e1_ref[...] = jnp.dot(perm_ref[...], e1, preferred_element_type=F32).astype(BF16).reshape(out3)


def _peer_rank(xnt, wq_t, skb, perm, tm):
    d, t_rows = xnt.shape
    qd = wq_t.shape[0]
    rows = N_KEYS * PEER_HEADS
    const = dict(pipeline_mode=pl.Buffered(1))
    return pl.pallas_call(
        _peer_rank_kernel,
        grid=(t_rows // tm,),
        in_specs=[
            pl.BlockSpec((d, tm), lambda i: (0, i)),
            pl.BlockSpec((qd, d), lambda i: (0, 0), **const),
            pl.BlockSpec((2, rows, rows), lambda i: (0, 0, 0), **const),
            pl.BlockSpec((rows, rows), lambda i: (0, 0), **const),
        ],
        out_specs=[
            pl.BlockSpec((N_KEYS, PEER_HEADS, tm), lambda i: (0, 0, i)),
            pl.BlockSpec((N_KEYS, PEER_HEADS, tm), lambda i: (0, 0, i)),
            pl.BlockSpec((PEER_HEADS, N_KEYS, tm), lambda i: (0, 0, i)),
            pl.BlockSpec((PEER_HEADS, N_KEYS, tm), lambda i: (0, 0, i)),
        ],
        out_shape=[
            jax.ShapeDtypeStruct((N_KEYS, PEER_HEADS, t_rows), F32),
            jax.ShapeDtypeStruct((N_KEYS, PEER_HEADS, t_rows), F32),
            jax.ShapeDtypeStruct((PEER_HEADS, N_KEYS, t_rows), BF16),
            jax.ShapeDtypeStruct((PEER_HEADS, N_KEYS, t_rows), BF16),
        ],
        scratch_shapes=[
            pltpu.VMEM((qd, tm), BF16),
            pltpu.VMEM((2, rows, tm), F32),
            pltpu.VMEM((N_KEYS, PEER_HEADS, tm), jnp.int32),
            pltpu.VMEM((N_KEYS, PEER_HEADS, tm), jnp.int32),
            pltpu.VMEM((TOPK, PEER_HEADS, tm), jnp.int32),
            pltpu.VMEM((TOPK, PEER_HEADS, tm), jnp.int32),
        ],
        compiler_params=_cparams(("parallel",)),
        name="peer_rank",
    )(xnt, wq_t, skb, perm)


def _peer_mix_kernel(xnt_ref, x1t_ref, u_ref, vt_ref, e0_ref, n0_ref, e1_ref, r1_ref, g_ref,
                     y_ref, acc_ref, a0_ref, a1_ref, *, eb, n_blocks):
    si = pl.program_id(1)
    n_i = eb // N_KEYS
    tm = acc_ref.shape[1]

    @pl.when(si == 0)
    def _():
        acc_ref[...] = jnp.zeros_like(acc_ref)
        a1_ref[...] = jnp.zeros_like(a1_ref)

    def step(cur_ref, prev_ref):
        blk = jnp.maximum(si - 1, 0)
        tiles = []
        for ii in range(n_i):
            i = blk * n_i + ii
            shape = (N_KEYS, tm)
            wgt = jnp.zeros(shape, BF16)
            n_rows = n0_ref[i]
            e_rows = e0_ref[i]
            for h in range(PEER_HEADS):
                n_tile = jnp.broadcast_to(n_rows[h:h + 1], shape).astype(BF16)
                e_tile = jnp.broadcast_to(e_rows[h:h + 1], shape).astype(BF16)
                wgt = wgt + e_tile * jnp.where(r1_ref[h] <= n_tile, e1_ref[h], jnp.zeros(shape, BF16))
            rows = slice(ii * N_KEYS, (ii + 1) * N_KEYS)
            tiles.append(jax.nn.gelu(prev_ref[rows, :]).astype(BF16) * wgt)
        act = jnp.concatenate(tiles, axis=0)
        cur_ref[...] = jnp.dot(u_ref[...], xnt_ref[...], preferred_element_type=F32)
        acc_ref[...] += jnp.dot(vt_ref[...], act, preferred_element_type=F32)

    parity = lax.rem(si, 2)

    @pl.when(parity == 0)
    def _():
        step(a0_ref, a1_ref)

    @pl.when(parity == 1)
    def _():
        step(a1_ref, a0_ref)

    @pl.when(si == n_blocks)
    def _():
        x2 = (x1t_ref[...] + acc_ref[...]).T
        ms = jnp.mean(x2 * x2, axis=-1, keepdims=True)
        y_ref[...] = x2 * lax.rsqrt(ms + EPS) * g_ref[...]


def _peer_mix(xnt, x1t, u, vt, e0, n0, e1, r1, g, tm, eb):
    d, t_rows = xnt.shape
    nb = u.shape[0] // eb
    head_spec = pl.BlockSpec((PEER_HEADS, N_KEYS, tm), lambda t, s: (0, 0, t))
    key_spec = pl.BlockSpec((N_KEYS, PEER_HEADS, tm), lambda t, s: (0, 0, t))
    return pl.pallas_call(
        functools.partial(_peer_mix_kernel, eb=eb, n_blocks=nb),
        grid=(t_rows // tm, nb + 1),
        in_specs=[
            pl.BlockSpec((d, tm), lambda t, s: (0, t)),
            pl.BlockSpec((d, tm), lambda t, s: (0, t)),
            pl.BlockSpec((eb, d), lambda t, s: (jnp.minimum(s, nb - 1), 0)),
            pl.BlockSpec((d, eb), lambda t, s: (0, jnp.maximum(s - 1, 0))),
            key_spec, key_spec, head_spec, head_spec,
            pl.BlockSpec((1, d), lambda t, s: (0, 0)),
        ],
        out_specs=pl.BlockSpec((tm, d), lambda t, s: (t, 0)),
        out_shape=jax.ShapeDtypeStruct((t_rows, d), F32),
        scratch_shapes=[pltpu.VMEM((d, tm), F32), pltpu.VMEM((eb, tm), F32), pltpu.VMEM((eb, tm), F32)],
        compiler_params=_cparams(("parallel", "arbitrary")),
        name="peer_mix",
    )(xnt, x1t, u, vt, e0, n0, e1, r1, g)


def kernel(x, norm_mix, w_in, conv_w, conv_b, ml_gate_bias, da_lambda, da_subln, ml_headnorm,
           w_branch_a, w_branch_b, w_out, norm_ffn, peer_query, peer_subkeys, expert_u, expert_v,
           norm_final):
    batch, seq, d = x.shape
    t_rows = batch * seq
    assert norm_mix.shape[0] == 1, "single-layer trunk"
    assert seq % CHUNK == 0 and d % LANES_V7X == 0
    ts = _tiles(t_rows, seq, d)
    x2 = x.reshape(t_rows, d)

    w = w_in[0]
    gate0 = N_PLAIN_COLS + N_GATE_LOGITS
    w_cat = jnp.concatenate([w[:, :N_PLAIN_COLS], w[:, gate0:]], axis=1).astype(BF16)
    w_if = jnp.pad(w[:, N_PLAIN_COLS:gate0], ((0, 0), (0, LANES_V7X - N_GATE_LOGITS))).astype(BF16)
    gbias = jnp.pad(ml_gate_bias[0], (0, LANES_V7X - N_GATE_LOGITS)).reshape(1, LANES_V7X)
    slopes = 2.0 ** (-8.0 * jnp.arange(1, DA_HEADS + 1, dtype=F32) / DA_HEADS)
    slopes = jnp.broadcast_to(slopes[:, None, None], (DA_HEADS, 1, LANES_V7X))

    proj, gates = _inproj(x2, norm_mix[0].reshape(1, d), w_cat, w_if, ts["proj_tm"], ts["proj_tn"])
    branch_a = _attention(proj, slopes, da_lambda[0], da_subln[0].reshape(HEAD_W, 1), batch, seq, ts["attn_tq"])
    branch_b = _mlstm(proj, gates, gbias, conv_w[0], conv_b[0].reshape(1, -1),
                      ml_headnorm[0].reshape(1, BRANCH_W), batch, seq, ts["ml_len"])
    merged = _merge(branch_a, branch_b, w_branch_a[0].astype(BF16), w_branch_b[0].astype(BF16),
                    proj, d, ts["merge_tm"], ts["merge_tn"])
    x1t, xnt = _outproj(x2, merged, w_out[0].astype(BF16), norm_ffn[0].reshape(1, d), ts["out_tm"])
    half_dim = peer_subkeys.shape[-1]
    wq_t = peer_query[0].reshape(d, PEER_HEADS, 2, half_dim).transpose(2, 1, 3, 0)
    wq_t = wq_t.reshape(2 * PEER_HEADS * half_dim, d).astype(BF16)
    eye = jnp.eye(PEER_HEADS, dtype=peer_subkeys.dtype)
    skb = jnp.einsum("hckd,hg->ckhgd", peer_subkeys[0], eye)
    skb = skb.reshape(2, N_KEYS * PEER_HEADS, PEER_HEADS * half_dim).astype(BF16)
    src_row = jnp.arange(N_KEYS * PEER_HEADS)
    perm = (src_row[None, :] == ((src_row % N_KEYS) * PEER_HEADS + src_row // N_KEYS)[:, None]).astype(BF16)
    e0, n0, e1, r1 = _peer_rank(xnt, wq_t, skb, perm, ts["peer_tm"])
    y = _peer_mix(xnt, x1t, expert_u[0].astype(BF16), expert_v[0].T.astype(BF16), e0, n0, e1, r1,
                  norm_final.reshape(1, d), ts["peer_tm"], ts["peer_eb"])
    return y.reshape(batch, seq, d)
```

```python
import functools

import jax
import jax.numpy as jnp
from jax import lax
from jax.experimental import pallas as pl
from jax.experimental.pallas import tpu as pltpu

F32 = jnp.float32
BF16 = jnp.bfloat16

CHUNK = 64
EPS = 1e-6
LAMBDA_INIT = 0.8 - 0.6
DA_HEADS = 8
DA_QK = 64
HEAD_W = 128
BRANCH_W = DA_HEADS * HEAD_W
ML_HEADS = 8
CONV_W = 4
PEER_HEADS = 8
N_KEYS = 128
TOPK = 16
N_PLAIN_COLS = 7 * BRANCH_W
N_GATE_LOGITS = 2 * ML_HEADS
LOG2E = 1.4426950408889634
QUERY_SCALE = DA_QK ** -0.5 * LOG2E

LANES_V7X = 128
SUBLANES_V7X = 8
VMEM_LIMIT_V7X = 56 * 1024 * 1024

NEG_INF = float("-inf")


def _cparams(sem):
    return pltpu.CompilerParams(dimension_semantics=sem, vmem_limit_bytes=VMEM_LIMIT_V7X)


def _tiles(t_rows, seq, d_model):
    return dict(
        proj_tm=min(1024, t_rows), proj_tn=1024 if d_model % 1024 == 0 else 512,
        attn_tq=min(512, seq),
        ml_len=min(256, seq),
        merge_tm=min(512, t_rows), merge_tn=min(512, d_model),
        out_tm=min(256, t_rows),
        peer_tm=min(512, t_rows), peer_eb=512,
    )


def _inproj_kernel(x_ref, g_ref, w_ref, wif_ref, proj_ref, if_ref, h_ref, *, n_query, n_plain):
    j = pl.program_id(1)

    @pl.when(j == 0)
    def _():
        x = x_ref[...]
        ms = jnp.mean(x * x, axis=-1, keepdims=True)
        h_ref[...] = (x * lax.rsqrt(ms + EPS) * g_ref[...]).astype(BF16)
        if_ref[...] = jnp.dot(h_ref[...], wif_ref[...], preferred_element_type=F32)

    acc = jnp.dot(h_ref[...], w_ref[...], preferred_element_type=F32)

    @pl.when(j < n_query)
    def _():
        proj_ref[...] = (acc * QUERY_SCALE).astype(BF16)

    @pl.when((j >= n_query) & (j < n_plain))
    def _():
        proj_ref[...] = acc.astype(BF16)

    @pl.when(j >= n_plain)
    def _():
        proj_ref[...] = jax.nn.sigmoid(acc).astype(BF16)


def _inproj(x2, g, w_cat, w_if, tm, tn):
    t_rows, d = x2.shape
    n = w_cat.shape[1]
    return pl.pallas_call(
        functools.partial(_inproj_kernel, n_query=BRANCH_W // tn, n_plain=N_PLAIN_COLS // tn),
        grid=(t_rows // tm, n // tn),
        in_specs=[
            pl.BlockSpec((tm, d), lambda i, j: (i, 0)),
            pl.BlockSpec((1, d), lambda i, j: (0, 0)),
            pl.BlockSpec((d, tn), lambda i, j: (0, j)),
            pl.BlockSpec((d, LANES_V7X), lambda i, j: (0, 0)),
        ],
        out_specs=[
            pl.BlockSpec((tm, tn), lambda i, j: (i, j)),
            pl.BlockSpec((tm, LANES_V7X), lambda i, j: (i, 0)),
        ],
        out_shape=[
            jax.ShapeDtypeStruct((t_rows, n), BF16),
            jax.ShapeDtypeStruct((t_rows, LANES_V7X), F32),
        ],
        scratch_shapes=[pltpu.VMEM((tm, d), BF16)],
        compiler_params=_cparams(("parallel", "arbitrary")),
        name="inproj",
    )(x2, g, w_cat, w_if)


def _attn_kernel(slope_ref, lam_ref, subg_ref, q_ref, k_ref, v_ref, o_ref,
                 kpos_ref, qpos_ref, bdiag_ref, ta_ref, tb_ref, tmax_a_ref, tmax_b_ref,
                 m_ref, l_ref, acc_ref, *, tq):
    qi = pl.program_id(2)
    slope = slope_ref[0:1, 0:1] * LOG2E
    cols = 2 * tq
    split = 16

    @pl.when(qi == 0)
    def _():
        c = lax.broadcasted_iota(jnp.int32, (tq, cols), 0)
        r = lax.broadcasted_iota(jnp.int32, (tq, cols), 1)
        r = jnp.where(r >= tq, r - tq, r)
        allowed = (c // CHUNK) <= (r // CHUNK)
        bdiag_ref[...] = jnp.where(allowed, -slope * jnp.abs(r - c).astype(F32), NEG_INF)
        s_a = slope.astype(BF16).astype(F32)
        s_b = (slope - s_a).astype(BF16).astype(F32)
        s_c = (slope - s_a - s_b).astype(BF16).astype(F32)
        lane = lax.broadcasted_iota(jnp.int32, (tq, HEAD_W), 1)
        off = lax.broadcasted_iota(jnp.int32, (tq, HEAD_W), 0)
        hi = (off // split).astype(F32)
        lo = (off % split).astype(F32)
        piece = jnp.where(lane % 3 == 0, s_a, jnp.where(lane % 3 == 1, s_b, s_c))
        zero = jnp.zeros((tq, HEAD_W), F32)
        kp = jnp.where(lane < 3, hi, jnp.where(lane < 6, lo, jnp.where(lane < 9, split * piece,
                                                                       jnp.where(lane < 12, piece, zero))))
        qp = jnp.where(lane < 3, split * piece, jnp.where(lane < 6, piece, jnp.where(lane < 9, -hi,
                                                                                     jnp.where(lane < 12, -lo, zero))))
        kpos_ref[...] = kp.astype(BF16)
        qpos_ref[0:tq, :] = qp.astype(BF16)
        qpos_ref[tq:cols, :] = qp.astype(BF16)

    q = q_ref[...]
    lane = lax.broadcasted_iota(jnp.int32, (tq, HEAD_W), 1)
    zero = jnp.zeros_like(q)
    q2 = jnp.concatenate([jnp.where(lane < DA_QK, q, zero), jnp.where(lane >= DA_QK, q, zero)], axis=0)
    q2_pos = jnp.concatenate([q2, qpos_ref[...]], axis=1)

    m_ref[...] = jnp.full((1, cols), NEG_INF, F32)
    l_ref[...] = jnp.zeros((1, cols), F32)
    acc_ref[...] = jnp.zeros((HEAD_W, cols), F32)
    nt = (((1,), (1,)), ((), ()))

    def softmax_step(t, t_max, v, shift_const):
        m_old = m_ref[...]
        m_new = jnp.maximum(m_old, t_max + shift_const)
        p = jnp.exp2(t - (m_new - shift_const))
        alpha = jnp.exp2(m_old - m_new)
        l_ref[...] = alpha * l_ref[...] + jnp.sum(p, axis=0, keepdims=True)
        pv = lax.dot_general(v, p.astype(BF16), (((0,), (0,)), ((), ())), preferred_element_type=F32)
        acc_ref[...] = alpha * acc_ref[...] + pv
        m_ref[...] = m_new

    last = jnp.maximum(qi - 1, 0)

    def scores(blk, t_ref, tmax_ref):
        k0 = pl.multiple_of(jnp.minimum(blk, last) * tq, tq)
        k_pos = jnp.concatenate([k_ref[pl.ds(k0, tq), :], kpos_ref[...]], axis=1)
        t = lax.dot_general(k_pos, q2_pos, nt, preferred_element_type=F32)
        t_ref[...] = t
        tmax_ref[...] = jnp.max(t, axis=0, keepdims=True)

    def consume(blk, t_ref, tmax_ref):
        kb = jnp.minimum(blk, last)
        k0 = pl.multiple_of(kb * tq, tq)
        shift = jnp.where(blk < qi, -slope * ((qi - kb) * tq).astype(F32), NEG_INF)
        softmax_step(t_ref[...], tmax_ref[...], v_ref[pl.ds(k0, tq), :], shift)

    scores(0, ta_ref, tmax_a_ref)
    kd = pl.multiple_of(qi * tq, tq)
    s = lax.dot_general(k_ref[pl.ds(kd, tq), :], q2, nt, preferred_element_type=F32) + bdiag_ref[...]
    softmax_step(s, jnp.max(s, axis=0, keepdims=True), v_ref[pl.ds(kd, tq), :], jnp.zeros((1, 1), F32))

    def body(i, carry):
        scores(2 * i + 1, tb_ref, tmax_b_ref)
        consume(2 * i, ta_ref, tmax_a_ref)
        scores(2 * i + 2, ta_ref, tmax_a_ref)
        consume(2 * i + 1, tb_ref, tmax_b_ref)
        return carry

    lax.fori_loop(0, (qi + 1) // 2, body, 0)

    lq = lam_ref[...]
    lam = (jnp.exp(jnp.sum(lq[0:1] * lq[1:2], keepdims=True))
           - jnp.exp(jnp.sum(lq[2:3] * lq[3:4], keepdims=True)) + LAMBDA_INIT)
    o = acc_ref[...] / l_ref[...]
    a = o[:, :tq] - lam * o[:, tq:]
    ms = jnp.mean(a * a, axis=0, keepdims=True)
    y = a * lax.rsqrt(ms + EPS) * subg_ref[...]
    o_ref[...] = (y * (1.0 - LAMBDA_INIT)).T.astype(BF16)


def _attention(proj, slopes, da_lambda, subg, batch, seq, tq):
    t_rows = proj.shape[0]
    nq = seq // tq
    return pl.pallas_call(
        functools.partial(_attn_kernel, tq=tq),
        grid=(batch, DA_HEADS, nq),
        in_specs=[
            pl.BlockSpec((None, 1, LANES_V7X), lambda b, h, i: (h, 0, 0)),
            pl.BlockSpec((4, DA_QK), lambda b, h, i: (0, 0)),
            pl.BlockSpec((HEAD_W, 1), lambda b, h, i: (0, 0)),
            pl.BlockSpec((tq, HEAD_W), lambda b, h, i: (b * nq + i, h)),
            pl.BlockSpec((seq, HEAD_W), lambda b, h, i: (b, DA_HEADS + h)),
            pl.BlockSpec((seq, HEAD_W), lambda b, h, i: (b, 2 * DA_HEADS + h)),
        ],
        out_specs=pl.BlockSpec((tq, HEAD_W), lambda b, h, i: (b * nq + i, h)),
        out_shape=jax.ShapeDtypeStruct((t_rows, BRANCH_W), BF16),
        scratch_shapes=[
            pltpu.VMEM((tq, HEAD_W), BF16),
            pltpu.VMEM((2 * tq, HEAD_W), BF16),
            pltpu.VMEM((tq, 2 * tq), F32),
            pltpu.VMEM((tq, 2 * tq), F32),
            pltpu.VMEM((tq, 2 * tq), F32),
            pltpu.VMEM((1, 2 * tq), F32),
            pltpu.VMEM((1, 2 * tq), F32),
            pltpu.VMEM((1, 2 * tq), F32),
            pltpu.VMEM((1, 2 * tq), F32),
            pltpu.VMEM((HEAD_W, 2 * tq), F32),
        ],
        compiler_params=_cparams(("parallel", "parallel", "arbitrary")),
        name="diff_attention",
    )(slopes, da_lambda, subg, proj, proj, proj)


def _mlstm_kernel(q_ref, k_ref, v_ref, o_ref, gate_ref, gbias_ref, cw_ref, cb_ref, hn_ref,
                  out_ref, c_ref, n_ref, m_ref, qext_ref, kext_ref, *, length):
    ci = pl.program_id(1)
    halo = SUBLANES_V7X

    @pl.when(ci == 0)
    def _():
        c_ref[...] = jnp.zeros_like(c_ref)
        n_ref[...] = jnp.zeros_like(n_ref)
        m_ref[...] = jnp.zeros_like(m_ref)
        qext_ref[0:halo, :] = jnp.zeros((halo, BRANCH_W), F32)
        kext_ref[0:halo, :] = jnp.zeros((halo, BRANCH_W), F32)

    def conv_silu(x_ref, ext_ref, col0):
        ext_ref[halo:, :] = x_ref[...].astype(F32)
        y = cb_ref[:, col0:col0 + BRANCH_W]
        for j in range(CONV_W):
            y = y + cw_ref[j:j + 1, col0:col0 + BRANCH_W] * ext_ref[pl.ds(halo - (CONV_W - 1) + j, length), :]
        ext_ref[0:halo, :] = ext_ref[length:length + halo, :]
        return y * jax.nn.sigmoid(y)

    qc = conv_silu(q_ref, qext_ref, 0).astype(BF16)
    kc = conv_silu(k_ref, kext_ref, BRANCH_W) * (HEAD_W ** -0.5)
    kc_b = kc.astype(BF16)

    g = gate_ref[...] + gbias_ref[...]
    lane = lax.broadcasted_iota(jnp.int32, (length, LANES_V7X), 1)
    lg = jnp.where(lane < ML_HEADS, g, jax.nn.log_sigmoid(g))
    lgt = lg.T
    r = lax.broadcasted_iota(jnp.int32, (length, length), 0)
    c = lax.broadcasted_iota(jnp.int32, (length, length), 1)
    causal = r >= c
    tril = causal.astype(F32)
    triu = (r <= c).astype(F32)
    hi = lax.Precision.HIGHEST
    bcol_all = jnp.dot(tril, lg, precision=hi, preferred_element_type=F32)
    brow_all = jnp.dot(lgt[ML_HEADS:2 * ML_HEADS], triu, precision=hi, preferred_element_type=F32)
    lirow_all = lgt[0:ML_HEADS]

    for h in range(ML_HEADS):
        cs = slice(h * HEAD_W, (h + 1) * HEAD_W)
        b_col = bcol_all[:, ML_HEADS + h:ML_HEADS + h + 1]
        li_col = lg[:, h:h + 1]
        b_row = brow_all[h:h + 1, :]
        li_row = lirow_all[h:h + 1, :]
        m_prev = m_ref[h:h + 1, 0:1]
        n_row = n_ref[h:h + 1, :]
        cmat = c_ref[h]
        qh = qc[:, cs]
        kh = kc[:, cs]
        vh = v_ref[:, cs]

        d = jnp.where(causal, b_col - b_row + li_row, NEG_INF)
        inter = b_col + m_prev
        m_t = jnp.maximum(jnp.max(d, axis=-1, keepdims=True), inter)
        w = jnp.exp(d - m_t)
        sc = jnp.exp(inter - m_t)
        s = lax.dot_general(qh, kc_b[:, cs], (((1,), (1,)), ((), ())), preferred_element_type=F32)
        wqk = w * s
        num = (sc * jnp.dot(qh, cmat.astype(BF16), preferred_element_type=F32)
               + jnp.dot(wqk.astype(BF16), vh, preferred_element_type=F32))
        den = (sc * jnp.sum(qh.astype(F32) * n_row, axis=-1, keepdims=True)
               + jnp.sum(wqk, axis=-1, keepdims=True))
        den = jnp.maximum(jnp.abs(den), jnp.exp(-m_t))
        hh = num / den

        b_last = b_col[length - 1:length, :]
        g_col = b_last - b_col + li_col
        m_new = jnp.maximum(b_last + m_prev, jnp.max(g_col, axis=0, keepdims=True))
        decay = jnp.exp(b_last + m_prev - m_new)
        kw = kh * jnp.exp(g_col - m_new)
        c_ref[h] = decay * cmat + lax.dot_general(
            kw.astype(BF16), vh, (((0,), (0,)), ((), ())), preferred_element_type=F32)
        n_ref[h:h + 1, :] = decay * n_row + jnp.sum(kw, axis=0, keepdims=True)
        m_ref[h:h + 1, :] = jnp.broadcast_to(m_new, (1, LANES_V7X))

        ms = jnp.mean(hh * hh, axis=-1, keepdims=True)
        hn = hh * lax.rsqrt(ms + EPS) * hn_ref[:, cs]
        out_ref[:, cs] = (jax.nn.sigmoid(o_ref[:, cs].astype(F32)) * hn).astype(BF16)


def _mlstm(proj, gates, gbias, conv_w, conv_b, headnorm, batch, seq, length):
    t_rows = proj.shape[0]
    nc = seq // length
    col = lambda k: pl.BlockSpec((length, BRANCH_W), lambda b, c: (b * nc + c, k))
    const = lambda shape: pl.BlockSpec(shape, lambda b, c: (0, 0))
    return pl.pallas_call(
        functools.partial(_mlstm_kernel, length=length),
        grid=(batch, nc),
        in_specs=[
            col(3), col(4), col(5), col(6),
            pl.BlockSpec((length, LANES_V7X), lambda b, c: (b * nc + c, 0)),
            const((1, LANES_V7X)),
            const((CONV_W, 2 * BRANCH_W)),
            const((1, 2 * BRANCH_W)),
            const((1, BRANCH_W)),
        ],
        out_specs=pl.BlockSpec((length, BRANCH_W), lambda b, c: (b * nc + c, 0)),
        out_shape=jax.ShapeDtypeStruct((t_rows, BRANCH_W), BF16),
        scratch_shapes=[
            pltpu.VMEM((ML_HEADS, HEAD_W, HEAD_W), F32),
            pltpu.VMEM((ML_HEADS, HEAD_W), F32),
            pltpu.VMEM((ML_HEADS, LANES_V7X), F32),
            pltpu.VMEM((SUBLANES_V7X + length, BRANCH_W), F32),
            pltpu.VMEM((SUBLANES_V7X + length, BRANCH_W), F32),
        ],
        compiler_params=_cparams(("parallel", "arbitrary")),
        name="mlstm",
    )(proj, proj, proj, proj, gates, gbias, conv_w, conv_b, headnorm)


def _merge_kernel(a_ref, b_ref, wa_ref, wb_ref, ga_ref, gb_ref, out_ref):
    pa = jnp.dot(a_ref[...], wa_ref[...], preferred_element_type=F32)
    pb = jnp.dot(b_ref[...], wb_ref[...], preferred_element_type=F32)
    out_ref[...] = (ga_ref[...].astype(F32) * pa + gb_ref[...].astype(F32) * pb).astype(BF16)


def _merge(branch_a, branch_b, wa, wb, proj, d_model, tm, tn):
    t_rows = branch_a.shape[0]
    ga0 = N_PLAIN_COLS // tn
    gb0 = (N_PLAIN_COLS + d_model) // tn
    return pl.pallas_call(
        _merge_kernel,
        grid=(t_rows // tm, d_model // tn),
        in_specs=[
            pl.BlockSpec((tm, BRANCH_W), lambda i, j: (i, 0)),
            pl.BlockSpec((tm, BRANCH_W), lambda i, j: (i, 0)),
            pl.BlockSpec((BRANCH_W, tn), lambda i, j: (0, j)),
            pl.BlockSpec((BRANCH_W, tn), lambda i, j: (0, j)),
            pl.BlockSpec((tm, tn), lambda i, j: (i, ga0 + j)),
            pl.BlockSpec((tm, tn), lambda i, j: (i, gb0 + j)),
        ],
        out_specs=pl.BlockSpec((tm, tn), lambda i, j: (i, j)),
        out_shape=jax.ShapeDtypeStruct((t_rows, d_model), BF16),
        compiler_params=_cparams(("parallel", "arbitrary")),
        name="merge",
    )(branch_a, branch_b, wa, wb, proj, proj)


def _outproj_kernel(x_ref, mg_ref, w_ref, g_ref, x1t_ref, xnt_ref):
    x1 = x_ref[...] + jnp.dot(mg_ref[...], w_ref[...], preferred_element_type=F32)
    ms = jnp.mean(x1 * x1, axis=-1, keepdims=True)
    xn = x1 * lax.rsqrt(ms + EPS) * g_ref[...]
    x1t_ref[...] = x1.T
    xnt_ref[...] = xn.T.astype(BF16)


def _outproj(x2, merged, w_out, g, tm):
    t_rows, d = x2.shape
    return pl.pallas_call(
        _outproj_kernel,
        grid=(t_rows // tm,),
        in_specs=[
            pl.BlockSpec((tm, d), lambda i: (i, 0)),
            pl.BlockSpec((tm, d), lambda i: (i, 0)),
            pl.BlockSpec((d, d), lambda i: (0, 0)),
            pl.BlockSpec((1, d), lambda i: (0, 0)),
        ],
        out_specs=[
            pl.BlockSpec((d, tm), lambda i: (0, i)),
            pl.BlockSpec((d, tm), lambda i: (0, i)),
        ],
        out_shape=[
            jax.ShapeDtypeStruct((d, t_rows), F32),
            jax.ShapeDtypeStruct((d, t_rows), BF16),
        ],
        compiler_params=_cparams(("parallel",)),
        name="outproj",
    )(x2, merged, w_out, g)


INT_MIN = -2 ** 31


def _sortable(x):
    bits = lax.bitcast_convert_type(x, jnp.int32)
    return bits ^ ((bits >> 31) & 0x7FFFFFFF)


def _unsortable(k):
    return lax.bitcast_convert_type(k ^ ((k >> 31) & 0x7FFFFFFF), F32)


def _top16_ranks(s_ref, shape3, key_ref, val_ref):
    def load_keys():
        key_ref[...] = _sortable(s_ref[...].reshape(shape3) + 0.0)

    def run(extract):
        load_keys()
        lax.fori_loop(0, TOPK, extract, jnp.max(key_ref[...], axis=0))

    def take_all(r, mx):
        val_ref[r] = mx
        cur = key_ref[...]
        cur = jnp.where(cur == mx[None], INT_MIN + 1 + r, cur)
        key_ref[...] = cur
        return jnp.max(cur, axis=0)

    def take_first(r, mx):
        val_ref[r] = mx
        cur = key_ref[...]
        kidx = lax.broadcasted_iota(jnp.int32, shape3, 0)
        first = jnp.min(jnp.where(cur == mx[None], kidx, N_KEYS), axis=0)
        cur = jnp.where(kidx == first[None], INT_MIN + 1 + r, cur)
        key_ref[...] = cur
        return jnp.max(cur, axis=0)

    run(take_all)
    taken = jnp.sum((key_ref[...] <= INT_MIN + TOPK).astype(F32), axis=0)

    @pl.when(jnp.max(taken) > float(TOPK))
    def _():
        run(take_first)

    cur = key_ref[...]
    return jnp.where(cur <= INT_MIN + TOPK, cur - INT_MIN, 0)


def _staircase(a, b):
    ridx = lax.broadcasted_iota(jnp.int32, a.shape, 0)
    top = a[0] + b[0]
    cnt = jnp.zeros(a.shape, jnp.int32)
    nxt = jnp.broadcast_to(b[0][None], a.shape)
    z = jnp.zeros(top.shape, F32)
    for _ in range(TOPK):
        f = a + nxt
        mx = jnp.max(f, axis=0)
        first = jnp.min(jnp.where(f == mx[None], ridx, TOPK), axis=0)
        sel = ridx == first[None]
        cnt = jnp.where(sel, cnt + 1, cnt)
        taken = jnp.max(jnp.where(sel, cnt, 0), axis=0)
        b_next = jnp.full(top.shape, NEG_INF, F32)
        for c in range(1, TOPK):
            b_next = jnp.where(taken == c, b[c], b_next)
        nxt = jnp.where(sel, b_next[None], nxt)
        z = z + jnp.exp(mx - top)
    return cnt, z


def _peer_rank_kernel(xnt_ref, wq_ref, skb_ref, perm_ref, e0_ref, n0_ref, e1_ref, r1_ref,
                      qt_ref, s_ref, key_ref, rank0_ref, a_ref, b_ref):
    tm = xnt_ref.shape[1]
    rows = N_KEYS * PEER_HEADS
    shape3 = (N_KEYS, PEER_HEADS, tm)
    qt_ref[...] = jnp.dot(wq_ref[...], xnt_ref[...], preferred_element_type=F32).astype(BF16)
    for c in range(2):
        s_ref[c] = jnp.dot(skb_ref[c], qt_ref[c * rows:(c + 1) * rows, :], preferred_element_type=F32)

    rank0_ref[...] = _top16_ranks(s_ref.at[0], shape3, key_ref, a_ref)
    rank1 = _top16_ranks(s_ref.at[1], shape3, key_ref, b_ref)
    a = _unsortable(a_ref[...])
    b = _unsortable(b_ref[...])
    cnt, z = _staircase(a, b)

    rank0 = rank0_ref[...]
    n0 = jnp.zeros(shape3, jnp.int32)
    for r in range(TOPK):
        n0 = jnp.where(rank0 == r + 1, cnt[r][None], n0)
    n0_ref[...] = n0.astype(F32)
    e0_ref[...] = jnp.exp(s_ref[0].reshape(shape3) - a[0][None]) * (1.0 / z)[None]

    r1 = jnp.where(rank1 == 0, 2 * TOPK, rank1).astype(F32).astype(BF16).reshape(rows, tm)
    e1 = jnp.exp(s_ref[1].reshape(shape3) - b[0][None]).astype(BF16).reshape(rows, tm)
    out3 = (PEER_HEADS, N_KEYS, tm)
    r1_ref[...] = jnp.dot(perm_ref[...], r1, preferred_element_type=F32).astype(BF16).reshape(out3)
    e1_ref[...] = jnp.dot(perm_ref[...], e1, preferred_element_type=F32).astype(BF16).reshape(out3)


def _peer_rank(xnt, wq_t, skb, perm, tm):
    d, t_rows = xnt.shape
    qd = wq_t.shape[0]
    rows = N_KEYS * PEER_HEADS
    const = dict(pipeline_mode=pl.Buffered(1))
    return pl.pallas_call(
        _peer_rank_kernel,
        grid=(t_rows // tm,),
        in_specs=[
            pl.BlockSpec((d, tm), lambda i: (0, i)),
            pl.BlockSpec((qd, d), lambda i: (0, 0), **const),
            pl.BlockSpec((2, rows, rows), lambda i: (0, 0, 0), **const),
            pl.BlockSpec((rows, rows), lambda i: (0, 0), **const),
        ],
        out_specs=[
            pl.BlockSpec((N_KEYS, PEER_HEADS, tm), lambda i: (0, 0, i)),
            pl.BlockSpec((N_KEYS, PEER_HEADS, tm), lambda i: (0, 0, i)),
            pl.BlockSpec((PEER_HEADS, N_KEYS, tm), lambda i: (0, 0, i)),
            pl.BlockSpec((PEER_HEADS, N_KEYS, tm), lambda i: (0, 0, i)),
        ],
        out_shape=[
            jax.ShapeDtypeStruct((N_KEYS, PEER_HEADS, t_rows), F32),
            jax.ShapeDtypeStruct((N_KEYS, PEER_HEADS, t_rows), F32),
            jax.ShapeDtypeStruct((PEER_HEADS, N_KEYS, t_rows), BF16),
            jax.ShapeDtypeStruct((PEER_HEADS, N_KEYS, t_rows), BF16),
        ],
        scratch_shapes=[
            pltpu.VMEM((qd, tm), BF16),
            pltpu.VMEM((2, rows, tm), F32),
            pltpu.VMEM((N_KEYS, PEER_HEADS, tm), jnp.int32),
            pltpu.VMEM((N_KEYS, PEER_HEADS, tm), jnp.int32),
            pltpu.VMEM((TOPK, PEER_HEADS, tm), jnp.int32),
            pltpu.VMEM((TOPK, PEER_HEADS, tm), jnp.int32),
        ],
        compiler_params=_cparams(("parallel",)),
        name="peer_rank",
    )(xnt, wq_t, skb, perm)


def _peer_mix_kernel(xnt_ref, x1t_ref, u_ref, vt_ref, e0_ref, n0_ref, e1_ref, r1_ref, g_ref,
                     y_ref, acc_ref, a0_ref, a1_ref, *, eb, n_blocks):
    si = pl.program_id(1)
    n_i = eb // N_KEYS
    tm = acc_ref.shape[1]

    @pl.when(si == 0)
    def _():
        acc_ref[...] = jnp.zeros_like(acc_ref)
        a1_ref[...] = jnp.zeros_like(a1_ref)

    def step(cur_ref, prev_ref):
        blk = jnp.maximum(si - 1, 0)
        tiles = []
        for ii in range(n_i):
            i = blk * n_i + ii
            shape = (N_KEYS, tm)
            wgt = jnp.zeros(shape, BF16)
            n_rows = n0_ref[i]
            e_rows = e0_ref[i]
            for h in range(PEER_HEADS):
                n_tile = jnp.broadcast_to(n_rows[h:h + 1], shape).astype(BF16)
                e_tile = jnp.broadcast_to(e_rows[h:h + 1], shape).astype(BF16)
                wgt = wgt + e_tile * jnp.where(r1_ref[h] <= n_tile, e1_ref[h], jnp.zeros(shape, BF16))
            rows = slice(ii * N_KEYS, (ii + 1) * N_KEYS)
            tiles.append(jax.nn.gelu(prev_ref[rows, :].astype(BF16)) * wgt)
        act = jnp.concatenate(tiles, axis=0)
        cur_ref[...] = jnp.dot(u_ref[...], xnt_ref[...], preferred_element_type=F32)
        acc_ref[...] += jnp.dot(vt_ref[...], act, preferred_element_type=F32)

    parity = lax.rem(si, 2)

    @pl.when(parity == 0)
    def _():
        step(a0_ref, a1_ref)

    @pl.when(parity == 1)
    def _():
        step(a1_ref, a0_ref)

    @pl.when(si == n_blocks)
    def _():
        x2 = (x1t_ref[...] + acc_ref[...]).T
        ms = jnp.mean(x2 * x2, axis=-1, keepdims=True)
        y_ref[...] = x2 * lax.rsqrt(ms + EPS) * g_ref[...]


def _peer_mix(xnt, x1t, u, vt, e0, n0, e1, r1, g, tm, eb):
    d, t_rows = xnt.shape
    nb = u.shape[0] // eb
    head_spec = pl.BlockSpec((PEER_HEADS, N_KEYS, tm), lambda t, s: (0, 0, t))
    key_spec = pl.BlockSpec((N_KEYS, PEER_HEADS, tm), lambda t, s: (0, 0, t))
    return pl.pallas_call(
        functools.partial(_peer_mix_kernel, eb=eb, n_blocks=nb),
        grid=(t_rows // tm, nb + 1),
        in_specs=[
            pl.BlockSpec((d, tm), lambda t, s: (0, t)),
            pl.BlockSpec((d, tm), lambda t, s: (0, t)),
            pl.BlockSpec((eb, d), lambda t, s: (jnp.minimum(s, nb - 1), 0)),
            pl.BlockSpec((d, eb), lambda t, s: (0, jnp.maximum(s - 1, 0))),
            key_spec, key_spec, head_spec, head_spec,
            pl.BlockSpec((1, d), lambda t, s: (0, 0)),
        ],
        out_specs=pl.BlockSpec((tm, d), lambda t, s: (t, 0)),
        out_shape=jax.ShapeDtypeStruct((t_rows, d), F32),
        scratch_shapes=[pltpu.VMEM((d, tm), F32), pltpu.VMEM((eb, tm), F32), pltpu.VMEM((eb, tm), F32)],
        compiler_params=_cparams(("parallel", "arbitrary")),
        name="peer_mix",
    )(xnt, x1t, u, vt, e0, n0, e1, r1, g)


def kernel(x, norm_mix, w_in, conv_w, conv_b, ml_gate_bias, da_lambda, da_subln, ml_headnorm,
           w_branch_a, w_branch_b, w_out, norm_ffn, peer_query, peer_subkeys, expert_u, expert_v,
           norm_final):
    batch, seq, d = x.shape
    t_rows = batch * seq
    assert norm_mix.shape[0] == 1, "single-layer trunk"
    assert seq % CHUNK == 0 and d % LANES_V7X == 0
    ts = _tiles(t_rows, seq, d)
    x2 = x.reshape(t_rows, d)

    w = w_in[0]
    gate0 = N_PLAIN_COLS + N_GATE_LOGITS
    w_cat = jnp.concatenate([w[:, :N_PLAIN_COLS], w[:, gate0:]], axis=1).astype(BF16)
    w_if = jnp.pad(w[:, N_PLAIN_COLS:gate0], ((0, 0), (0, LANES_V7X - N_GATE_LOGITS))).astype(BF16)
    gbias = jnp.pad(ml_gate_bias[0], (0, LANES_V7X - N_GATE_LOGITS)).reshape(1, LANES_V7X)
    slopes = 2.0 ** (-8.0 * jnp.arange(1, DA_HEADS + 1, dtype=F32) / DA_HEADS)
    slopes = jnp.broadcast_to(slopes[:, None, None], (DA_HEADS, 1, LANES_V7X))

    proj, gates = _inproj(x2, norm_mix[0].reshape(1, d), w_cat, w_if, ts["proj_tm"], ts["proj_tn"])
    branch_a = _attention(proj, slopes, da_lambda[0], da_subln[0].reshape(HEAD_W, 1), batch, seq, ts["attn_tq"])
    branch_b = _mlstm(proj, gates, gbias, conv_w[0], conv_b[0].reshape(1, -1),
                      ml_headnorm[0].reshape(1, BRANCH_W), batch, seq, ts["ml_len"])
    merged = _merge(branch_a, branch_b, w_branch_a[0].astype(BF16), w_branch_b[0].astype(BF16),
                    proj, d, ts["merge_tm"], ts["merge_tn"])
    x1t, xnt = _outproj(x2, merged, w_out[0].astype(BF16), norm_ffn[0].reshape(1, d), ts["out_tm"])
    half_dim = peer_subkeys.shape[-1]
    wq_t = peer_query[0].reshape(d, PEER_HEADS, 2, half_dim).transpose(2, 1, 3, 0)
    wq_t = wq_t.reshape(2 * PEER_HEADS * half_dim, d).astype(BF16)
    eye = jnp.eye(PEER_HEADS, dtype=peer_subkeys.dtype)
    skb = jnp.einsum("hckd,hg->ckhgd", peer_subkeys[0], eye)
    skb = skb.reshape(2, N_KEYS * PEER_HEADS, PEER_HEADS * half_dim).astype(BF16)
    src_row = jnp.arange(N_KEYS * PEER_HEADS)
    perm = (src_row[None, :] == ((src_row % N_KEYS) * PEER_HEADS + src_row // N_KEYS)[:, None]).astype(BF16)
    e0, n0, e1, r1 = _peer_rank(xnt, wq_t, skb, perm, ts["peer_tm"])
    y = _peer_mix(xnt, x1t, expert_u[0].astype(BF16), expert_v[0].T.astype(BF16), e0, n0, e1, r1,
                  norm_final.reshape(1, d), ts["peer_tm"], ts["peer_eb"])
    return y.reshape(batch, seq, d)
```

```python
import functools

import jax
import jax.numpy as jnp
from jax import lax
from jax.experimental import pallas as pl
from jax.experimental.pallas import tpu as pltpu

F32 = jnp.float32
BF16 = jnp.bfloat16

CHUNK = 64
EPS = 1e-6
LAMBDA_INIT = 0.8 - 0.6
DA_HEADS = 8
DA_QK = 64
HEAD_W = 128
BRANCH_W = DA_HEADS * HEAD_W
ML_HEADS = 8
CONV_W = 4
PEER_HEADS = 8
N_KEYS = 128
TOPK = 16
N_PLAIN_COLS = 7 * BRANCH_W
N_GATE_LOGITS = 2 * ML_HEADS
LOG2E = 1.4426950408889634
QUERY_SCALE = DA_QK ** -0.5 * LOG2E

LANES_V7X = 128
SUBLANES_V7X = 8
MXU_WIDTH_V7X = 256
ATTN_STRIP = MXU_WIDTH_V7X
VMEM_LIMIT_V7X = 56 * 1024 * 1024

NEG_INF = float("-inf")


def _cparams(sem):
    return pltpu.CompilerParams(dimension_semantics=sem, vmem_limit_bytes=VMEM_LIMIT_V7X)


def _tiles(t_rows, seq, d_model):
    return dict(
        proj_tm=min(1024, t_rows), proj_tn=1024 if d_model % 1024 == 0 else 512,
        attn_tq=min(512, seq),
        ml_len=min(256, seq),
        merge_tm=min(512, t_rows), merge_tn=min(512, d_model),
        out_tm=min(256, t_rows),
        peer_tm=min(512, t_rows), peer_eb=1024,
    )


def _inproj_kernel(x_ref, g_ref, w_ref, wif_ref, proj_ref, if_ref, h_ref, *, n_query, n_plain):
    j = pl.program_id(1)

    @pl.when(j == 0)
    def _():
        x = x_ref[...]
        ms = jnp.mean(x * x, axis=-1, keepdims=True)
        h_ref[...] = (x * lax.rsqrt(ms + EPS) * g_ref[...]).astype(BF16)
        if_ref[...] = jnp.dot(h_ref[...], wif_ref[...], preferred_element_type=F32)

    acc = jnp.dot(h_ref[...], w_ref[...], preferred_element_type=F32)

    @pl.when(j < n_query)
    def _():
        proj_ref[...] = (acc * QUERY_SCALE).astype(BF16)

    @pl.when((j >= n_query) & (j < n_plain))
    def _():
        proj_ref[...] = acc.astype(BF16)

    @pl.when(j >= n_plain)
    def _():
        proj_ref[...] = jax.nn.sigmoid(acc).astype(BF16)


def _inproj(x2, g, w_cat, w_if, tm, tn):
    t_rows, d = x2.shape
    n = w_cat.shape[1]
    return pl.pallas_call(
        functools.partial(_inproj_kernel, n_query=BRANCH_W // tn, n_plain=N_PLAIN_COLS // tn),
        grid=(t_rows // tm, n // tn),
        in_specs=[
            pl.BlockSpec((tm, d), lambda i, j: (i, 0)),
            pl.BlockSpec((1, d), lambda i, j: (0, 0)),
            pl.BlockSpec((d, tn), lambda i, j: (0, j)),
            pl.BlockSpec((d, LANES_V7X), lambda i, j: (0, 0)),
        ],
        out_specs=[
            pl.BlockSpec((tm, tn), lambda i, j: (i, j)),
            pl.BlockSpec((tm, LANES_V7X), lambda i, j: (i, 0)),
        ],
        out_shape=[
            jax.ShapeDtypeStruct((t_rows, n), BF16),
            jax.ShapeDtypeStruct((t_rows, LANES_V7X), F32),
        ],
        scratch_shapes=[pltpu.VMEM((tm, d), BF16)],
        compiler_params=_cparams(("parallel", "arbitrary")),
        name="inproj",
    )(x2, g, w_cat, w_if)


def _attn_kernel(slope_ref, lam_ref, subg_ref, q_ref, k_ref, v_ref, o_ref,
                 kpos_ref, qpos_ref, bdiag_ref, ta_ref, tb_ref, tmax_a_ref, tmax_b_ref,
                 m_ref, l_ref, acc_ref, *, tq):
    qi = pl.program_id(2)
    slope = slope_ref[0:1, 0:1] * LOG2E
    cols = 2 * tq
    split = 16

    @pl.when(qi == 0)
    def _():
        c = lax.broadcasted_iota(jnp.int32, (tq, cols), 0)
        r = lax.broadcasted_iota(jnp.int32, (tq, cols), 1)
        r = jnp.where(r >= tq, r - tq, r)
        allowed = (c // CHUNK) <= (r // CHUNK)
        bdiag_ref[...] = jnp.where(allowed, -slope * jnp.abs(r - c).astype(F32), NEG_INF)
        s_a = slope.astype(BF16).astype(F32)
        s_b = (slope - s_a).astype(BF16).astype(F32)
        s_c = (slope - s_a - s_b).astype(BF16).astype(F32)
        lane = lax.broadcasted_iota(jnp.int32, (tq, HEAD_W), 1)
        off = lax.broadcasted_iota(jnp.int32, (tq, HEAD_W), 0)
        hi = (off // split).astype(F32)
        lo = (off % split).astype(F32)
        piece = jnp.where(lane % 3 == 0, s_a, jnp.where(lane % 3 == 1, s_b, s_c))
        zero = jnp.zeros((tq, HEAD_W), F32)
        kp = jnp.where(lane < 3, hi, jnp.where(lane < 6, lo, jnp.where(lane < 9, split * piece,
                                                                       jnp.where(lane < 12, piece, zero))))
        qp = jnp.where(lane < 3, split * piece, jnp.where(lane < 6, piece, jnp.where(lane < 9, -hi,
                                                                                     jnp.where(lane < 12, -lo, zero))))
        kpos_ref[...] = kp.astype(BF16)
        qpos_ref[0:tq, :] = qp.astype(BF16)
        qpos_ref[tq:cols, :] = qp.astype(BF16)

    q = q_ref[...]
    lane = lax.broadcasted_iota(jnp.int32, (tq, HEAD_W), 1)
    zero = jnp.zeros_like(q)
    q2 = jnp.concatenate([jnp.where(lane < DA_QK, q, zero), jnp.where(lane >= DA_QK, q, zero)], axis=0)
    q2_pos = jnp.concatenate([q2, qpos_ref[...]], axis=1)

    m_ref[...] = jnp.full((1, cols), NEG_INF, F32)
    l_ref[...] = jnp.zeros((1, cols), F32)
    acc_ref[...] = jnp.zeros((HEAD_W, cols), F32)
    nt = (((1,), (1,)), ((), ()))

    def softmax_step(t_ref, tmax_ref, v, shift_const):
        for c0 in range(0, cols, ATTN_STRIP):
            cs = slice(c0, c0 + ATTN_STRIP)
            m_old = m_ref[:, cs]
            m_new = jnp.maximum(m_old, tmax_ref[:, cs] + shift_const)
            p = jnp.exp2(t_ref[:, cs] - (m_new - shift_const))
            alpha = jnp.exp2(m_old - m_new)
            l_ref[:, cs] = alpha * l_ref[:, cs] + jnp.sum(p, axis=0, keepdims=True)
            pv = lax.dot_general(v, p.astype(BF16), (((0,), (0,)), ((), ())), preferred_element_type=F32)
            acc_ref[:, cs] = alpha * acc_ref[:, cs] + pv
            m_ref[:, cs] = m_new

    last = jnp.maximum(qi - 1, 0)

    def scores(seq, t_ref, tmax_ref):
        k0 = pl.multiple_of(jnp.clip(seq - 1, 0, last) * tq, tq)
        k_pos = jnp.concatenate([k_ref[pl.ds(k0, tq), :], kpos_ref[...]], axis=1)
        t = lax.dot_general(k_pos, q2_pos, nt, preferred_element_type=F32)
        t_ref[...] = t
        tmax_ref[...] = jnp.max(t, axis=0, keepdims=True)

    def consume(seq, t_ref, tmax_ref):
        kb = jnp.where(seq == 0, qi, jnp.clip(seq - 1, 0, last))
        k0 = pl.multiple_of(kb * tq, tq)
        shift = jnp.where(seq <= qi, -slope * ((qi - kb) * tq).astype(F32), NEG_INF)
        softmax_step(t_ref, tmax_ref, v_ref[pl.ds(k0, tq), :], shift)

    kd = pl.multiple_of(qi * tq, tq)
    s = lax.dot_general(k_ref[pl.ds(kd, tq), :], q2, nt, preferred_element_type=F32) + bdiag_ref[...]
    ta_ref[...] = s
    tmax_a_ref[...] = jnp.max(s, axis=0, keepdims=True)

    def body(i, carry):
        scores(2 * i + 1, tb_ref, tmax_b_ref)
        consume(2 * i, ta_ref, tmax_a_ref)
        scores(2 * i + 2, ta_ref, tmax_a_ref)
        consume(2 * i + 1, tb_ref, tmax_b_ref)
        return carry

    lax.fori_loop(0, (qi + 2) // 2, body, 0)

    lq = lam_ref[...]
    lam = (jnp.exp(jnp.sum(lq[0:1] * lq[1:2], keepdims=True))
           - jnp.exp(jnp.sum(lq[2:3] * lq[3:4], keepdims=True)) + LAMBDA_INIT)
    o = acc_ref[...] / l_ref[...]
    a = o[:, :tq] - lam * o[:, tq:]
    ms = jnp.mean(a * a, axis=0, keepdims=True)
    y = a * lax.rsqrt(ms + EPS) * subg_ref[...]
    o_ref[...] = (y * (1.0 - LAMBDA_INIT)).T.astype(BF16)


def _attention(proj, slopes, da_lambda, subg, batch, seq, tq):
    t_rows = proj.shape[0]
    nq = seq // tq
    return pl.pallas_call(
        functools.partial(_attn_kernel, tq=tq),
        grid=(batch, DA_HEADS, nq),
        in_specs=[
            pl.BlockSpec((None, 1, LANES_V7X), lambda b, h, i: (h, 0, 0)),
            pl.BlockSpec((4, DA_QK), lambda b, h, i: (0, 0)),
            pl.BlockSpec((HEAD_W, 1), lambda b, h, i: (0, 0)),
            pl.BlockSpec((tq, HEAD_W), lambda b, h, i: (b * nq + i, h)),
            pl.BlockSpec((seq, HEAD_W), lambda b, h, i: (b, DA_HEADS + h)),
            pl.BlockSpec((seq, HEAD_W), lambda b, h, i: (b, 2 * DA_HEADS + h)),
        ],
        out_specs=pl.BlockSpec((tq, HEAD_W), lambda b, h, i: (b * nq + i, h)),
        out_shape=jax.ShapeDtypeStruct((t_rows, BRANCH_W), BF16),
        scratch_shapes=[
            pltpu.VMEM((tq, HEAD_W), BF16),
            pltpu.VMEM((2 * tq, HEAD_W), BF16),
            pltpu.VMEM((tq, 2 * tq), F32),
            pltpu.VMEM((tq, 2 * tq), F32),
            pltpu.VMEM((tq, 2 * tq), F32),
            pltpu.VMEM((1, 2 * tq), F32),
            pltpu.VMEM((1, 2 * tq), F32),
            pltpu.VMEM((1, 2 * tq), F32),
            pltpu.VMEM((1, 2 * tq), F32),
            pltpu.VMEM((HEAD_W, 2 * tq), F32),
        ],
        compiler_params=_cparams(("parallel", "parallel", "arbitrary")),
        name="diff_attention",
    )(slopes, da_lambda, subg, proj, proj, proj)


def _mlstm_kernel(q_ref, k_ref, v_ref, o_ref, gate_ref, gbias_ref, cw_ref, cb_ref, hn_ref,
                  out_ref, c_ref, n_ref, m_ref, qext_ref, kext_ref, *, length):
    ci = pl.program_id(1)
    halo = SUBLANES_V7X

    @pl.when(ci == 0)
    def _():
        c_ref[...] = jnp.zeros_like(c_ref)
        n_ref[...] = jnp.zeros_like(n_ref)
        m_ref[...] = jnp.zeros_like(m_ref)
        qext_ref[0:halo, :] = jnp.zeros((halo, BRANCH_W), F32)
        kext_ref[0:halo, :] = jnp.zeros((halo, BRANCH_W), F32)

    def conv_silu(x_ref, ext_ref, col0):
        ext_ref[halo:, :] = x_ref[...].astype(F32)
        y = cb_ref[:, col0:col0 + BRANCH_W]
        for j in range(CONV_W):
            y = y + cw_ref[j:j + 1, col0:col0 + BRANCH_W] * ext_ref[pl.ds(halo - (CONV_W - 1) + j, length), :]
        ext_ref[0:halo, :] = ext_ref[length:length + halo, :]
        return y * jax.nn.sigmoid(y)

    qc = conv_silu(q_ref, qext_ref, 0).astype(BF16)
    kc = conv_silu(k_ref, kext_ref, BRANCH_W) * (HEAD_W ** -0.5)
    kc_b = kc.astype(BF16)

    g = gate_ref[...] + gbias_ref[...]
    lane = lax.broadcasted_iota(jnp.int32, (length, LANES_V7X), 1)
    lg = jnp.where(lane < ML_HEADS, g, jax.nn.log_sigmoid(g))
    lgt = lg.T
    r = lax.broadcasted_iota(jnp.int32, (length, length), 0)
    c = lax.broadcasted_iota(jnp.int32, (length, length), 1)
    causal = r >= c
    tril = causal.astype(F32)
    triu = (r <= c).astype(F32)
    hi = lax.Precision.HIGHEST
    bcol_all = jnp.dot(tril, lg, precision=hi, preferred_element_type=F32)
    brow_all = jnp.dot(lgt[ML_HEADS:2 * ML_HEADS], triu, precision=hi, preferred_element_type=F32)
    lirow_all = lgt[0:ML_HEADS]

    for h in range(ML_HEADS):
        cs = slice(h * HEAD_W, (h + 1) * HEAD_W)
        b_col = bcol_all[:, ML_HEADS + h:ML_HEADS + h + 1]
        li_col = lg[:, h:h + 1]
        b_row = brow_all[h:h + 1, :]
        li_row = lirow_all[h:h + 1, :]
        m_prev = m_ref[h:h + 1, 0:1]
        n_row = n_ref[h:h + 1, :]
        cmat = c_ref[h]
        qh = qc[:, cs]
        kh = kc[:, cs]
        vh = v_ref[:, cs]

        d = jnp.where(causal, b_col - b_row + li_row, NEG_INF)
        inter = b_col + m_prev
        m_t = jnp.maximum(jnp.max(d, axis=-1, keepdims=True), inter)
        w = jnp.exp(d - m_t)
        sc = jnp.exp(inter - m_t)
        s = lax.dot_general(qh, kc_b[:, cs], (((1,), (1,)), ((), ())), preferred_element_type=F32)
        wqk = w * s
        num = (sc * jnp.dot(qh, cmat.astype(BF16), preferred_element_type=F32)
               + jnp.dot(wqk.astype(BF16), vh, preferred_element_type=F32))
        den = (sc * jnp.sum(qh.astype(F32) * n_row, axis=-1, keepdims=True)
               + jnp.sum(wqk, axis=-1, keepdims=True))
        den = jnp.maximum(jnp.abs(den), jnp.exp(-m_t))
        hh = num / den

        b_last = b_col[length - 1:length, :]
        g_col = b_last - b_col + li_col
        m_new = jnp.maximum(b_last + m_prev, jnp.max(g_col, axis=0, keepdims=True))
        decay = jnp.exp(b_last + m_prev - m_new)
        kw = kh * jnp.exp(g_col - m_new)
        c_ref[h] = decay * cmat + lax.dot_general(
            kw.astype(BF16), vh, (((0,), (0,)), ((), ())), preferred_element_type=F32)
        n_ref[h:h + 1, :] = decay * n_row + jnp.sum(kw, axis=0, keepdims=True)
        m_ref[h:h + 1, :] = jnp.broadcast_to(m_new, (1, LANES_V7X))

        ms = jnp.mean(hh * hh, axis=-1, keepdims=True)
        hn = hh * lax.rsqrt(ms + EPS) * hn_ref[:, cs]
        out_ref[:, cs] = (jax.nn.sigmoid(o_ref[:, cs].astype(F32)) * hn).astype(BF16)


def _mlstm(proj, gates, gbias, conv_w, conv_b, headnorm, batch, seq, length):
    t_rows = proj.shape[0]
    nc = seq // length
    col = lambda k: pl.BlockSpec((length, BRANCH_W), lambda b, c: (b * nc + c, k))
    const = lambda shape: pl.BlockSpec(shape, lambda b, c: (0, 0))
    return pl.pallas_call(
        functools.partial(_mlstm_kernel, length=length),
        grid=(batch, nc),
        in_specs=[
            col(3), col(4), col(5), col(6),
            pl.BlockSpec((length, LANES_V7X), lambda b, c: (b * nc + c, 0)),
            const((1, LANES_V7X)),
            const((CONV_W, 2 * BRANCH_W)),
            const((1, 2 * BRANCH_W)),
            const((1, BRANCH_W)),
        ],
        out_specs=pl.BlockSpec((length, BRANCH_W), lambda b, c: (b * nc + c, 0)),
        out_shape=jax.ShapeDtypeStruct((t_rows, BRANCH_W), BF16),
        scratch_shapes=[
            pltpu.VMEM((ML_HEADS, HEAD_W, HEAD_W), F32),
            pltpu.VMEM((ML_HEADS, HEAD_W), F32),
            pltpu.VMEM((ML_HEADS, LANES_V7X), F32),
            pltpu.VMEM((SUBLANES_V7X + length, BRANCH_W), F32),
            pltpu.VMEM((SUBLANES_V7X + length, BRANCH_W), F32),
        ],
        compiler_params=_cparams(("parallel", "arbitrary")),
        name="mlstm",
    )(proj, proj, proj, proj, gates, gbias, conv_w, conv_b, headnorm)


def _merge_kernel(a_ref, b_ref, wa_ref, wb_ref, ga_ref, gb_ref, out_ref):
    pa = jnp.dot(a_ref[...], wa_ref[...], preferred_element_type=F32)
    pb = jnp.dot(b_ref[...], wb_ref[...], preferred_element_type=F32)
    out_ref[...] = (ga_ref[...].astype(F32) * pa + gb_ref[...].astype(F32) * pb).astype(BF16)


def _merge(branch_a, branch_b, wa, wb, proj, d_model, tm, tn):
    t_rows = branch_a.shape[0]
    ga0 = N_PLAIN_COLS // tn
    gb0 = (N_PLAIN_COLS + d_model) // tn
    return pl.pallas_call(
        _merge_kernel,
        grid=(t_rows // tm, d_model // tn),
        in_specs=[
            pl.BlockSpec((tm, BRANCH_W), lambda i, j: (i, 0)),
            pl.BlockSpec((tm, BRANCH_W), lambda i, j: (i, 0)),
            pl.BlockSpec((BRANCH_W, tn), lambda i, j: (0, j)),
            pl.BlockSpec((BRANCH_W, tn), lambda i, j: (0, j)),
            pl.BlockSpec((tm, tn), lambda i, j: (i, ga0 + j)),
            pl.BlockSpec((tm, tn), lambda i, j: (i, gb0 + j)),
        ],
        out_specs=pl.BlockSpec((tm, tn), lambda i, j: (i, j)),
        out_shape=jax.ShapeDtypeStruct((t_rows, d_model), BF16),
        compiler_params=_cparams(("parallel", "arbitrary")),
        name="merge",
    )(branch_a, branch_b, wa, wb, proj, proj)


def _outproj_kernel(x_ref, mg_ref, w_ref, g_ref, x1t_ref, xnt_ref):
    x1 = x_ref[...] + jnp.dot(mg_ref[...], w_ref[...], preferred_element_type=F32)
    ms = jnp.mean(x1 * x1, axis=-1, keepdims=True)
    xn = x1 * lax.rsqrt(ms + EPS) * g_ref[...]
    x1t_ref[...] = x1.T
    xnt_ref[...] = xn.T.astype(BF16)


def _outproj(x2, merged, w_out, g, tm):
    t_rows, d = x2.shape
    return pl.pallas_call(
        _outproj_kernel,
        grid=(t_rows // tm,),
        in_specs=[
            pl.BlockSpec((tm, d), lambda i: (i, 0)),
            pl.BlockSpec((tm, d), lambda i: (i, 0)),
            pl.BlockSpec((d, d), lambda i: (0, 0)),
            pl.BlockSpec((1, d), lambda i: (0, 0)),
        ],
        out_specs=[
            pl.BlockSpec((d, tm), lambda i: (0, i)),
            pl.BlockSpec((d, tm), lambda i: (0, i)),
        ],
        out_shape=[
            jax.ShapeDtypeStruct((d, t_rows), F32),
            jax.ShapeDtypeStruct((d, t_rows), BF16),
        ],
        compiler_params=_cparams(("parallel",)),
        name="outproj",
    )(x2, merged, w_out, g)


INT_MIN = -2 ** 31


def _sortable(x):
    bits = lax.bitcast_convert_type(x, jnp.int32)
    return bits ^ ((bits >> 31) & 0x7FFFFFFF)


def _unsortable(k):
    return lax.bitcast_convert_type(k ^ ((k >> 31) & 0x7FFFFFFF), F32)


def _top16_ranks(s_ref, shape3, key_ref, val_ref):
    def load_keys():
        key_ref[...] = _sortable(s_ref[...].reshape(shape3) + 0.0)

    def run(extract):
        load_keys()
        lax.fori_loop(0, TOPK, extract, jnp.max(key_ref[...], axis=0))

    def take_all(r, mx):
        val_ref[r] = mx
        cur = key_ref[...]
        cur = jnp.where(cur == mx[None], INT_MIN + 1 + r, cur)
        key_ref[...] = cur
        return jnp.max(cur, axis=0)

    def take_first(r, mx):
        val_ref[r] = mx
        cur = key_ref[...]
        kidx = lax.broadcasted_iota(jnp.int32, shape3, 0)
        first = jnp.min(jnp.where(cur == mx[None], kidx, N_KEYS), axis=0)
        cur = jnp.where(kidx == first[None], INT_MIN + 1 + r, cur)
        key_ref[...] = cur
        return jnp.max(cur, axis=0)

    run(take_all)
    taken = jnp.sum((key_ref[...] <= INT_MIN + TOPK).astype(F32), axis=0)

    @pl.when(jnp.max(taken) > float(TOPK))
    def _():
        run(take_first)

    cur = key_ref[...]
    return jnp.where(cur <= INT_MIN + TOPK, cur - INT_MIN, 0)


def _staircase(a, b):
    ridx = lax.broadcasted_iota(jnp.int32, a.shape, 0)
    top = a[0] + b[0]
    cnt = jnp.zeros(a.shape, jnp.int32)
    nxt = jnp.broadcast_to(b[0][None], a.shape)
    z = jnp.zeros(top.shape, F32)
    for _ in range(TOPK):
        f = a + nxt
        mx = jnp.max(f, axis=0)
        first = jnp.min(jnp.where(f == mx[None], ridx, TOPK), axis=0)
        sel = ridx == first[None]
        cnt = jnp.where(sel, cnt + 1, cnt)
        taken = jnp.max(jnp.where(sel, cnt, 0), axis=0)
        b_next = jnp.full(top.shape, NEG_INF, F32)
        for c in range(1, TOPK):
            b_next = jnp.where(taken == c, b[c], b_next)
        nxt = jnp.where(sel, b_next[None], nxt)
        z = z + jnp.exp(mx - top)
    return cnt, z


def _peer_rank_kernel(xnt_ref, wq_ref, skb_ref, perm_ref, e0_ref, n0_ref, e1_ref, r1_ref,
                      qt_ref, s_ref, key_ref, rank0_ref, a_ref, b_ref):
    tm = xnt_ref.shape[1]
    rows = N_KEYS * PEER_HEADS
    shape3 = (N_KEYS, PEER_HEADS, tm)
    qt_ref[...] = jnp.dot(wq_ref[...], xnt_ref[...], preferred_element_type=F32).astype(BF16)
    for c in range(2):
        s_ref[c] = jnp.dot(skb_ref[c], qt_ref[c * rows:(c + 1) * rows, :], preferred_element_type=F32)

    rank0_ref[...] = _top16_ranks(s_ref.at[0], shape3, key_ref, a_ref)
    rank1 = _top16_ranks(s_ref.at[1], shape3, key_ref, b_ref)
    a = _unsortable(a_ref[...])
    b = _unsortable(b_ref[...])
    cnt, z = _staircase(a, b)

    rank0 = rank0_ref[...]
    n0 = jnp.zeros(shape3, jnp.int32)
    for r in range(TOPK):
        n0 = jnp.where(rank0 == r + 1, cnt[r][None], n0)
    n0_ref[...] = n0.astype(F32)
    e0_ref[...] = jnp.exp(s_ref[0].reshape(shape3) - a[0][None]) * (1.0 / z)[None]

    r1 = jnp.where(rank1 == 0, 2 * TOPK, rank1).astype(F32).astype(BF16).reshape(rows, tm)
    e1 = jnp.exp(s_ref[1].reshape(shape3) - b[0][None]).astype(BF16).reshape(rows, tm)
    out3 = (PEER_HEADS, N_KEYS, tm)
    r1_ref[...] = jnp.dot(perm_ref[...], r1, preferred_element_type=F32).astype(BF16).reshape(out3)
    e1_ref[...] = jnp.dot(perm_ref[...], e1, preferred_element_type=F32).astype(BF16).reshape(out3)


def _peer_rank(xnt, wq_t, skb, perm, tm):
    d, t_rows = xnt.shape
    qd = wq_t.shape[0]
    rows = N_KEYS * PEER_HEADS
    const = dict(pipeline_mode=pl.Buffered(1))
    return pl.pallas_call(
        _peer_rank_kernel,
        grid=(t_rows // tm,),
        in_specs=[
            pl.BlockSpec((d, tm), lambda i: (0, i)),
            pl.BlockSpec((qd, d), lambda i: (0, 0), **const),
            pl.BlockSpec((2, rows, rows), lambda i: (0, 0, 0), **const),
            pl.BlockSpec((rows, rows), lambda i: (0, 0), **const),
        ],
        out_specs=[
            pl.BlockSpec((N_KEYS, PEER_HEADS, tm), lambda i: (0, 0, i)),
            pl.BlockSpec((N_KEYS, PEER_HEADS, tm), lambda i: (0, 0, i)),
            pl.BlockSpec((PEER_HEADS, N_KEYS, tm), lambda i: (0, 0, i)),
            pl.BlockSpec((PEER_HEADS, N_KEYS, tm), lambda i: (0, 0, i)),
        ],
        out_shape=[
            jax.ShapeDtypeStruct((N_KEYS, PEER_HEADS, t_rows), F32),
            jax.ShapeDtypeStruct((N_KEYS, PEER_HEADS, t_rows), F32),
            jax.ShapeDtypeStruct((PEER_HEADS, N_KEYS, t_rows), BF16),
            jax.ShapeDtypeStruct((PEER_HEADS, N_KEYS, t_rows), BF16),
        ],
        scratch_shapes=[
            pltpu.VMEM((qd, tm), BF16),
            pltpu.VMEM((2, rows, tm), F32),
            pltpu.VMEM((N_KEYS, PEER_HEADS, tm), jnp.int32),
            pltpu.VMEM((N_KEYS, PEER_HEADS, tm), jnp.int32),
            pltpu.VMEM((TOPK, PEER_HEADS, tm), jnp.int32),
            pltpu.VMEM((TOPK, PEER_HEADS, tm), jnp.int32),
        ],
        compiler_params=_cparams(("parallel",)),
        name="peer_rank",
    )(xnt, wq_t, skb, perm)


def _peer_mix_kernel(xnt_ref, x1t_ref, u_ref, vt_ref, e0_ref, n0_ref, e1_ref, r1_ref, g_ref,
                     y_ref, acc_ref, a0_ref, a1_ref, *, eb, n_blocks):
    si = pl.program_id(1)
    n_i = eb // N_KEYS
    tm = acc_ref.shape[1]

    @pl.when(si == 0)
    def _():
        acc_ref[...] = jnp.zeros_like(acc_ref)
        a1_ref[...] = jnp.zeros_like(a1_ref)

    def step(cur_ref, prev_ref):
        blk = jnp.maximum(si - 1, 0)
        tiles = []
        for ii in range(n_i):
            i = blk * n_i + ii
            shape = (N_KEYS, tm)
            wgt = jnp.zeros(shape, BF16)
            n_rows = n0_ref[i]
            e_rows = e0_ref[i]
            for h in range(PEER_HEADS):
                n_tile = jnp.broadcast_to(n_rows[h:h + 1], shape).astype(BF16)
                e_tile = jnp.broadcast_to(e_rows[h:h + 1], shape).astype(BF16)
                wgt = wgt + e_tile * jnp.where(r1_ref[h] <= n_tile, e1_ref[h], jnp.zeros(shape, BF16))
            rows = slice(ii * N_KEYS, (ii + 1) * N_KEYS)
            tiles.append(jax.nn.gelu(prev_ref[rows, :].astype(BF16)) * wgt)
        act = jnp.concatenate(tiles, axis=0)
        cur_ref[...] = jnp.dot(u_ref[...], xnt_ref[...], preferred_element_type=F32)
        acc_ref[...] += jnp.dot(vt_ref[...], act, preferred_element_type=F32)

    parity = lax.rem(si, 2)

    @pl.when(parity == 0)
    def _():
        step(a0_ref, a1_ref)

    @pl.when(parity == 1)
    def _():
        step(a1_ref, a0_ref)

    @pl.when(si == n_blocks)
    def _():
        x2 = (x1t_ref[...] + acc_ref[...]).T
        ms = jnp.mean(x2 * x2, axis=-1, keepdims=True)
        y_ref[...] = x2 * lax.rsqrt(ms + EPS) * g_ref[...]


def _peer_mix(xnt, x1t, u, vt, e0, n0, e1, r1, g, tm, eb):
    d, t_rows = xnt.shape
    nb = u.shape[0] // eb
    once = dict(pipeline_mode=pl.Buffered(1))
    head_spec = pl.BlockSpec((PEER_HEADS, N_KEYS, tm), lambda t, s: (0, 0, t))
    key_spec = pl.BlockSpec((N_KEYS, PEER_HEADS, tm), lambda t, s: (0, 0, t), **once)
    return pl.pallas_call(
        functools.partial(_peer_mix_kernel, eb=eb, n_blocks=nb),
        grid=(t_rows // tm, nb + 1),
        in_specs=[
            pl.BlockSpec((d, tm), lambda t, s: (0, t)),
            pl.BlockSpec((d, tm), lambda t, s: (0, t), **once),
            pl.BlockSpec((eb, d), lambda t, s: (jnp.minimum(s, nb - 1), 0)),
            pl.BlockSpec((d, eb), lambda t, s: (0, jnp.maximum(s - 1, 0))),
            key_spec, key_spec, head_spec, head_spec,
            pl.BlockSpec((1, d), lambda t, s: (0, 0)),
        ],
        out_specs=pl.BlockSpec((tm, d), lambda t, s: (t, 0)),
        out_shape=jax.ShapeDtypeStruct((t_rows, d), F32),
        scratch_shapes=[pltpu.VMEM((d, tm), F32), pltpu.VMEM((eb, tm), F32), pltpu.VMEM((eb, tm), F32)],
        compiler_params=_cparams(("parallel", "arbitrary")),
        name="peer_mix",
    )(xnt, x1t, u, vt, e0, n0, e1, r1, g)


def kernel(x, norm_mix, w_in, conv_w, conv_b, ml_gate_bias, da_lambda, da_subln, ml_headnorm,
           w_branch_a, w_branch_b, w_out, norm_ffn, peer_query, peer_subkeys, expert_u, expert_v,
           norm_final):
    batch, seq, d = x.shape
    t_rows = batch * seq
    assert norm_mix.shape[0] == 1, "single-layer trunk"
    assert seq % CHUNK == 0 and d % LANES_V7X == 0
    ts = _tiles(t_rows, seq, d)
    x2 = x.reshape(t_rows, d)

    w = w_in[0]
    gate0 = N_PLAIN_COLS + N_GATE_LOGITS
    w_cat = jnp.concatenate([w[:, :N_PLAIN_COLS], w[:, gate0:]], axis=1).astype(BF16)
    w_if = jnp.pad(w[:, N_PLAIN_COLS:gate0], ((0, 0), (0, LANES_V7X - N_GATE_LOGITS))).astype(BF16)
    gbias = jnp.pad(ml_gate_bias[0], (0, LANES_V7X - N_GATE_LOGITS)).reshape(1, LANES_V7X)
    slopes = 2.0 ** (-8.0 * jnp.arange(1, DA_HEADS + 1, dtype=F32) / DA_HEADS)
    slopes = jnp.broadcast_to(slopes[:, None, None], (DA_HEADS, 1, LANES_V7X))

    proj, gates = _inproj(x2, norm_mix[0].reshape(1, d), w_cat, w_if, ts["proj_tm"], ts["proj_tn"])
    branch_a = _attention(proj, slopes, da_lambda[0], da_subln[0].reshape(HEAD_W, 1), batch, seq, ts["attn_tq"])
    branch_b = _mlstm(proj, gates, gbias, conv_w[0], conv_b[0].reshape(1, -1),
                      ml_headnorm[0].reshape(1, BRANCH_W), batch, seq, ts["ml_len"])
    merged = _merge(branch_a, branch_b, w_branch_a[0].astype(BF16), w_branch_b[0].astype(BF16),
                    proj, d, ts["merge_tm"], ts["merge_tn"])
    x1t, xnt = _outproj(x2, merged, w_out[0].astype(BF16), norm_ffn[0].reshape(1, d), ts["out_tm"])
    half_dim = peer_subkeys.shape[-1]
    wq_t = peer_query[0].reshape(d, PEER_HEADS, 2, half_dim).transpose(2, 1, 3, 0)
    wq_t = wq_t.reshape(2 * PEER_HEADS * half_dim, d).astype(BF16)
    eye = jnp.eye(PEER_HEADS, dtype=peer_subkeys.dtype)
    skb = jnp.einsum("hckd,hg->ckhgd", peer_subkeys[0], eye)
    skb = skb.reshape(2, N_KEYS * PEER_HEADS, PEER_HEADS * half_dim).astype(BF16)
    src_row = jnp.arange(N_KEYS * PEER_HEADS)
    perm = (src_row[None, :] == ((src_row % N_KEYS) * PEER_HEADS + src_row // N_KEYS)[:, None]).astype(BF16)
    e0, n0, e1, r1 = _peer_rank(xnt, wq_t, skb, perm, ts["peer_tm"])
    y = _peer_mix(xnt, x1t, expert_u[0].astype(BF16), expert_v[0].T.astype(BF16), e0, n0, e1, r1,
                  norm_final.reshape(1, d), ts["peer_tm"], ts["peer_eb"])
    return y.reshape(batch, seq, d)
```

```python
import functools

import jax
import jax.numpy as jnp
from jax import lax
from jax.experimental import pallas as pl
from jax.experimental.pallas import tpu as pltpu

F32 = jnp.float32
BF16 = jnp.bfloat16

CHUNK = 64
EPS = 1e-6
LAMBDA_INIT = 0.8 - 0.6
DA_HEADS = 8
DA_QK = 64
HEAD_W = 128
BRANCH_W = DA_HEADS * HEAD_W
ML_HEADS = 8
CONV_W = 4
PEER_HEADS = 8
N_KEYS = 128
TOPK = 16
N_PLAIN_COLS = 7 * BRANCH_W
N_GATE_LOGITS = 2 * ML_HEADS
LOG2E = 1.4426950408889634
QUERY_SCALE = DA_QK ** -0.5 * LOG2E

LANES_V7X = 128
SUBLANES_V7X = 8
MXU_WIDTH_V7X = 256
ATTN_STRIP = MXU_WIDTH_V7X
VMEM_LIMIT_V7X = 56 * 1024 * 1024

NEG_INF = float("-inf")


def _cparams(sem):
    return pltpu.CompilerParams(dimension_semantics=sem, vmem_limit_bytes=VMEM_LIMIT_V7X)


def _tiles(t_rows, seq, d_model):
    return dict(
        proj_tm=min(1024, t_rows), proj_tn=1024 if d_model % 1024 == 0 else 512,
        attn_tq=min(512, seq),
        ml_len=min(256, seq),
        merge_tm=min(512, t_rows), merge_tn=min(512, d_model),
        out_tm=min(256, t_rows),
        peer_tm=min(512, t_rows), peer_eb=1024,
    )


def _inproj_kernel(x_ref, g_ref, w_ref, wif_ref, proj_ref, if_ref, h_ref, *, n_query, n_plain):
    j = pl.program_id(1)

    @pl.when(j == 0)
    def _():
        x = x_ref[...]
        ms = jnp.mean(x * x, axis=-1, keepdims=True)
        h_ref[...] = (x * lax.rsqrt(ms + EPS) * g_ref[...]).astype(BF16)
        if_ref[...] = jnp.dot(h_ref[...], wif_ref[...], preferred_element_type=F32)

    acc = jnp.dot(h_ref[...], w_ref[...], preferred_element_type=F32)

    @pl.when(j < n_query)
    def _():
        proj_ref[...] = (acc * QUERY_SCALE).astype(BF16)

    @pl.when((j >= n_query) & (j < n_plain))
    def _():
        proj_ref[...] = acc.astype(BF16)

    @pl.when(j >= n_plain)
    def _():
        proj_ref[...] = jax.nn.sigmoid(acc).astype(BF16)


def _inproj(x2, g, w_cat, w_if, tm, tn):
    t_rows, d = x2.shape
    n = w_cat.shape[1]
    return pl.pallas_call(
        functools.partial(_inproj_kernel, n_query=BRANCH_W // tn, n_plain=N_PLAIN_COLS // tn),
        grid=(t_rows // tm, n // tn),
        in_specs=[
            pl.BlockSpec((tm, d), lambda i, j: (i, 0)),
            pl.BlockSpec((1, d), lambda i, j: (0, 0)),
            pl.BlockSpec((d, tn), lambda i, j: (0, j)),
            pl.BlockSpec((d, LANES_V7X), lambda i, j: (0, 0)),
        ],
        out_specs=[
            pl.BlockSpec((tm, tn), lambda i, j: (i, j)),
            pl.BlockSpec((tm, LANES_V7X), lambda i, j: (i, 0)),
        ],
        out_shape=[
            jax.ShapeDtypeStruct((t_rows, n), BF16),
            jax.ShapeDtypeStruct((t_rows, LANES_V7X), F32),
        ],
        scratch_shapes=[pltpu.VMEM((tm, d), BF16)],
        compiler_params=_cparams(("parallel", "arbitrary")),
        name="inproj",
    )(x2, g, w_cat, w_if)


def _attn_kernel(slope_ref, lam_ref, subg_ref, q_ref, k_ref, v_ref, o_ref,
                 kpos_ref, qpos_ref, bdiag_ref, ta_ref, tb_ref, tmax_a_ref, tmax_b_ref,
                 m_ref, l_ref, acc_ref, *, tq):
    qi = pl.program_id(2)
    slope = slope_ref[0:1, 0:1] * LOG2E
    cols = 2 * tq
    split = 16

    @pl.when(qi == 0)
    def _():
        c = lax.broadcasted_iota(jnp.int32, (tq, cols), 0)
        r = lax.broadcasted_iota(jnp.int32, (tq, cols), 1)
        r = jnp.where(r >= tq, r - tq, r)
        allowed = (c // CHUNK) <= (r // CHUNK)
        bdiag_ref[...] = jnp.where(allowed, -slope * jnp.abs(r - c).astype(F32), NEG_INF)
        s_a = slope.astype(BF16).astype(F32)
        s_b = (slope - s_a).astype(BF16).astype(F32)
        s_c = (slope - s_a - s_b).astype(BF16).astype(F32)
        lane = lax.broadcasted_iota(jnp.int32, (tq, HEAD_W), 1)
        off = lax.broadcasted_iota(jnp.int32, (tq, HEAD_W), 0)
        hi = (off // split).astype(F32)
        lo = (off % split).astype(F32)
        piece = jnp.where(lane % 3 == 0, s_a, jnp.where(lane % 3 == 1, s_b, s_c))
        zero = jnp.zeros((tq, HEAD_W), F32)
        kp = jnp.where(lane < 3, hi, jnp.where(lane < 6, lo, jnp.where(lane < 9, split * piece,
                                                                       jnp.where(lane < 12, piece, zero))))
        qp = jnp.where(lane < 3, split * piece, jnp.where(lane < 6, piece, jnp.where(lane < 9, -hi,
                                                                                     jnp.where(lane < 12, -lo, zero))))
        kpos_ref[...] = kp.astype(BF16)
        qpos_ref[0:tq, :] = qp.astype(BF16)
        qpos_ref[tq:cols, :] = qp.astype(BF16)

    q = q_ref[...]
    lane = lax.broadcasted_iota(jnp.int32, (tq, HEAD_W), 1)
    zero = jnp.zeros_like(q)
    q2 = jnp.concatenate([jnp.where(lane < DA_QK, q, zero), jnp.where(lane >= DA_QK, q, zero)], axis=0)
    q2_pos = jnp.concatenate([q2, qpos_ref[...]], axis=1)

    m_ref[...] = jnp.full((1, cols), NEG_INF, F32)
    l_ref[...] = jnp.zeros((1, cols), F32)
    acc_ref[...] = jnp.zeros((HEAD_W, cols), F32)
    nt = (((1,), (1,)), ((), ()))

    def softmax_step(t_ref, tmax_ref, v, shift_const):
        for c0 in range(0, cols, ATTN_STRIP):
            cs = slice(c0, c0 + ATTN_STRIP)
            m_old = m_ref[:, cs]
            m_new = jnp.maximum(m_old, tmax_ref[:, cs] + shift_const)
            p = jnp.exp2(t_ref[:, cs] - (m_new - shift_const))
            alpha = jnp.exp2(m_old - m_new)
            l_ref[:, cs] = alpha * l_ref[:, cs] + jnp.sum(p, axis=0, keepdims=True)
            pv = lax.dot_general(v, p.astype(BF16), (((0,), (0,)), ((), ())), preferred_element_type=F32)
            acc_ref[:, cs] = alpha * acc_ref[:, cs] + pv
            m_ref[:, cs] = m_new

    def scores(entry, t_ref, tmax_ref):
        k0 = pl.multiple_of((entry - 1) * tq, tq)
        k_pos = jnp.concatenate([k_ref[pl.ds(k0, tq), :], kpos_ref[...]], axis=1)
        t = lax.dot_general(k_pos, q2_pos, nt, preferred_element_type=F32)
        t_ref[...] = t
        tmax_ref[...] = jnp.max(t, axis=0, keepdims=True)

    def consume(entry, t_ref, tmax_ref):
        kb = jnp.where(entry == 0, qi, entry - 1)
        k0 = pl.multiple_of(kb * tq, tq)
        softmax_step(t_ref, tmax_ref, v_ref[pl.ds(k0, tq), :], -slope * ((qi - kb) * tq).astype(F32))

    kd = pl.multiple_of(qi * tq, tq)
    s = lax.dot_general(k_ref[pl.ds(kd, tq), :], q2, nt, preferred_element_type=F32) + bdiag_ref[...]
    ta_ref[...] = s
    tmax_a_ref[...] = jnp.max(s, axis=0, keepdims=True)

    def body(i, carry):
        scores(2 * i + 1, tb_ref, tmax_b_ref)
        consume(2 * i, ta_ref, tmax_a_ref)
        scores(2 * i + 2, ta_ref, tmax_a_ref)
        consume(2 * i + 1, tb_ref, tmax_b_ref)
        return carry

    pairs = qi // 2
    lax.fori_loop(0, pairs, body, 0)
    tail = 2 * pairs

    @pl.when(qi % 2 == 1)
    def _():
        scores(tail + 1, tb_ref, tmax_b_ref)
        consume(tail, ta_ref, tmax_a_ref)
        consume(tail + 1, tb_ref, tmax_b_ref)

    @pl.when(qi % 2 == 0)
    def _():
        consume(tail, ta_ref, tmax_a_ref)

    lq = lam_ref[...]
    lam = (jnp.exp(jnp.sum(lq[0:1] * lq[1:2], keepdims=True))
           - jnp.exp(jnp.sum(lq[2:3] * lq[3:4], keepdims=True)) + LAMBDA_INIT)
    o = acc_ref[...] / l_ref[...]
    a = o[:, :tq] - lam * o[:, tq:]
    ms = jnp.mean(a * a, axis=0, keepdims=True)
    y = a * lax.rsqrt(ms + EPS) * subg_ref[...]
    o_ref[...] = (y * (1.0 - LAMBDA_INIT)).T.astype(BF16)


def _attention(proj, slopes, da_lambda, subg, batch, seq, tq):
    t_rows = proj.shape[0]
    nq = seq // tq
    return pl.pallas_call(
        functools.partial(_attn_kernel, tq=tq),
        grid=(batch, DA_HEADS, nq),
        in_specs=[
            pl.BlockSpec((None, 1, LANES_V7X), lambda b, h, i: (h, 0, 0)),
            pl.BlockSpec((4, DA_QK), lambda b, h, i: (0, 0)),
            pl.BlockSpec((HEAD_W, 1), lambda b, h, i: (0, 0)),
            pl.BlockSpec((tq, HEAD_W), lambda b, h, i: (b * nq + i, h)),
            pl.BlockSpec((seq, HEAD_W), lambda b, h, i: (b, DA_HEADS + h)),
            pl.BlockSpec((seq, HEAD_W), lambda b, h, i: (b, 2 * DA_HEADS + h)),
        ],
        out_specs=pl.BlockSpec((tq, HEAD_W), lambda b, h, i: (b * nq + i, h)),
        out_shape=jax.ShapeDtypeStruct((t_rows, BRANCH_W), BF16),
        scratch_shapes=[
            pltpu.VMEM((tq, HEAD_W), BF16),
            pltpu.VMEM((2 * tq, HEAD_W), BF16),
            pltpu.VMEM((tq, 2 * tq), F32),
            pltpu.VMEM((tq, 2 * tq), F32),
            pltpu.VMEM((tq, 2 * tq), F32),
            pltpu.VMEM((1, 2 * tq), F32),
            pltpu.VMEM((1, 2 * tq), F32),
            pltpu.VMEM((1, 2 * tq), F32),
            pltpu.VMEM((1, 2 * tq), F32),
            pltpu.VMEM((HEAD_W, 2 * tq), F32),
        ],
        compiler_params=_cparams(("parallel", "parallel", "arbitrary")),
        name="diff_attention",
    )(slopes, da_lambda, subg, proj, proj, proj)


def _mlstm_kernel(q_ref, k_ref, v_ref, o_ref, gate_ref, gbias_ref, cw_ref, cb_ref, hn_ref,
                  out_ref, c_ref, n_ref, m_ref, qext_ref, kext_ref, *, length):
    ci = pl.program_id(1)
    halo = SUBLANES_V7X

    @pl.when(ci == 0)
    def _():
        c_ref[...] = jnp.zeros_like(c_ref)
        n_ref[...] = jnp.zeros_like(n_ref)
        m_ref[...] = jnp.zeros_like(m_ref)
        qext_ref[0:halo, :] = jnp.zeros((halo, BRANCH_W), F32)
        kext_ref[0:halo, :] = jnp.zeros((halo, BRANCH_W), F32)

    def conv_silu(x_ref, ext_ref, col0):
        ext_ref[halo:, :] = x_ref[...].astype(F32)
        y = cb_ref[:, col0:col0 + BRANCH_W]
        for j in range(CONV_W):
            y = y + cw_ref[j:j + 1, col0:col0 + BRANCH_W] * ext_ref[pl.ds(halo - (CONV_W - 1) + j, length), :]
        ext_ref[0:halo, :] = ext_ref[length:length + halo, :]
        return y * jax.nn.sigmoid(y)

    qc = conv_silu(q_ref, qext_ref, 0).astype(BF16)
    kc = conv_silu(k_ref, kext_ref, BRANCH_W) * (HEAD_W ** -0.5)
    kc_b = kc.astype(BF16)

    g = gate_ref[...] + gbias_ref[...]
    lane = lax.broadcasted_iota(jnp.int32, (length, LANES_V7X), 1)
    lg = jnp.where(lane < ML_HEADS, g, jax.nn.log_sigmoid(g))
    lgt = lg.T
    r = lax.broadcasted_iota(jnp.int32, (length, length), 0)
    c = lax.broadcasted_iota(jnp.int32, (length, length), 1)
    causal = r >= c
    tril = causal.astype(F32)
    triu = (r <= c).astype(F32)
    hi = lax.Precision.HIGHEST
    bcol_all = jnp.dot(tril, lg, precision=hi, preferred_element_type=F32)
    brow_all = jnp.dot(lgt[ML_HEADS:2 * ML_HEADS], triu, precision=hi, preferred_element_type=F32)
    lirow_all = lgt[0:ML_HEADS]

    for h in range(ML_HEADS):
        cs = slice(h * HEAD_W, (h + 1) * HEAD_W)
        b_col = bcol_all[:, ML_HEADS + h:ML_HEADS + h + 1]
        li_col = lg[:, h:h + 1]
        b_row = brow_all[h:h + 1, :]
        li_row = lirow_all[h:h + 1, :]
        m_prev = m_ref[h:h + 1, 0:1]
        n_row = n_ref[h:h + 1, :]
        cmat = c_ref[h]
        qh = qc[:, cs]
        kh = kc[:, cs]
        vh = v_ref[:, cs]

        d = jnp.where(causal, b_col - b_row + li_row, NEG_INF)
        inter = b_col + m_prev
        m_t = jnp.maximum(jnp.max(d, axis=-1, keepdims=True), inter)
        w = jnp.exp(d - m_t)
        sc = jnp.exp(inter - m_t)
        s = lax.dot_general(qh, kc_b[:, cs], (((1,), (1,)), ((), ())), preferred_element_type=F32)
        wqk = w * s
        num = (sc * jnp.dot(qh, cmat.astype(BF16), preferred_element_type=F32)
               + jnp.dot(wqk.astype(BF16), vh, preferred_element_type=F32))
        den = (sc * jnp.sum(qh.astype(F32) * n_row, axis=-1, keepdims=True)
               + jnp.sum(wqk, axis=-1, keepdims=True))
        den = jnp.maximum(jnp.abs(den), jnp.exp(-m_t))
        hh = num / den

        b_last = b_col[length - 1:length, :]
        g_col = b_last - b_col + li_col
        m_new = jnp.maximum(b_last + m_prev, jnp.max(g_col, axis=0, keepdims=True))
        decay = jnp.exp(b_last + m_prev - m_new)
        kw = kh * jnp.exp(g_col - m_new)
        c_ref[h] = decay * cmat + lax.dot_general(
            kw.astype(BF16), vh, (((0,), (0,)), ((), ())), preferred_element_type=F32)
        n_ref[h:h + 1, :] = decay * n_row + jnp.sum(kw, axis=0, keepdims=True)
        m_ref[h:h + 1, :] = jnp.broadcast_to(m_new, (1, LANES_V7X))

        ms = jnp.mean(hh * hh, axis=-1, keepdims=True)
        hn = hh * lax.rsqrt(ms + EPS) * hn_ref[:, cs]
        out_ref[:, cs] = (jax.nn.sigmoid(o_ref[:, cs].astype(F32)) * hn).astype(BF16)


def _mlstm(proj, gates, gbias, conv_w, conv_b, headnorm, batch, seq, length):
    t_rows = proj.shape[0]
    nc = seq // length
    col = lambda k: pl.BlockSpec((length, BRANCH_W), lambda b, c: (b * nc + c, k))
    const = lambda shape: pl.BlockSpec(shape, lambda b, c: (0, 0))
    return pl.pallas_call(
        functools.partial(_mlstm_kernel, length=length),
        grid=(batch, nc),
        in_specs=[
            col(3), col(4), col(5), col(6),
            pl.BlockSpec((length, LANES_V7X), lambda b, c: (b * nc + c, 0)),
            const((1, LANES_V7X)),
            const((CONV_W, 2 * BRANCH_W)),
            const((1, 2 * BRANCH_W)),
            const((1, BRANCH_W)),
        ],
        out_specs=pl.BlockSpec((length, BRANCH_W), lambda b, c: (b * nc + c, 0)),
        out_shape=jax.ShapeDtypeStruct((t_rows, BRANCH_W), BF16),
        scratch_shapes=[
            pltpu.VMEM((ML_HEADS, HEAD_W, HEAD_W), F32),
            pltpu.VMEM((ML_HEADS, HEAD_W), F32),
            pltpu.VMEM((ML_HEADS, LANES_V7X), F32),
            pltpu.VMEM((SUBLANES_V7X + length, BRANCH_W), F32),
            pltpu.VMEM((SUBLANES_V7X + length, BRANCH_W), F32),
        ],
        compiler_params=_cparams(("parallel", "arbitrary")),
        name="mlstm",
    )(proj, proj, proj, proj, gates, gbias, conv_w, conv_b, headnorm)


def _merge_kernel(a_ref, b_ref, wa_ref, wb_ref, ga_ref, gb_ref, out_ref):
    pa = jnp.dot(a_ref[...], wa_ref[...], preferred_element_type=F32)
    pb = jnp.dot(b_ref[...], wb_ref[...], preferred_element_type=F32)
    out_ref[...] = (ga_ref[...].astype(F32) * pa + gb_ref[...].astype(F32) * pb).astype(BF16)


def _merge(branch_a, branch_b, wa, wb, proj, d_model, tm, tn):
    t_rows = branch_a.shape[0]
    ga0 = N_PLAIN_COLS // tn
    gb0 = (N_PLAIN_COLS + d_model) // tn
    return pl.pallas_call(
        _merge_kernel,
        grid=(t_rows // tm, d_model // tn),
        in_specs=[
            pl.BlockSpec((tm, BRANCH_W), lambda i, j: (i, 0)),
            pl.BlockSpec((tm, BRANCH_W), lambda i, j: (i, 0)),
            pl.BlockSpec((BRANCH_W, tn), lambda i, j: (0, j)),
            pl.BlockSpec((BRANCH_W, tn), lambda i, j: (0, j)),
            pl.BlockSpec((tm, tn), lambda i, j: (i, ga0 + j)),
            pl.BlockSpec((tm, tn), lambda i, j: (i, gb0 + j)),
        ],
        out_specs=pl.BlockSpec((tm, tn), lambda i, j: (i, j)),
        out_shape=jax.ShapeDtypeStruct((t_rows, d_model), BF16),
        compiler_params=_cparams(("parallel", "arbitrary")),
        name="merge",
    )(branch_a, branch_b, wa, wb, proj, proj)


def _outproj_kernel(x_ref, mg_ref, w_ref, g_ref, x1t_ref, xnt_ref):
    x1 = x_ref[...] + jnp.dot(mg_ref[...], w_ref[...], preferred_element_type=F32)
    ms = jnp.mean(x1 * x1, axis=-1, keepdims=True)
    xn = x1 * lax.rsqrt(ms + EPS) * g_ref[...]
    x1t_ref[...] = x1.T
    xnt_ref[...] = xn.T.astype(BF16)


def _outproj(x2, merged, w_out, g, tm):
    t_rows, d = x2.shape
    return pl.pallas_call(
        _outproj_kernel,
        grid=(t_rows // tm,),
        in_specs=[
            pl.BlockSpec((tm, d), lambda i: (i, 0)),
            pl.BlockSpec((tm, d), lambda i: (i, 0)),
            pl.BlockSpec((d, d), lambda i: (0, 0)),
            pl.BlockSpec((1, d), lambda i: (0, 0)),
        ],
        out_specs=[
            pl.BlockSpec((d, tm), lambda i: (0, i)),
            pl.BlockSpec((d, tm), lambda i: (0, i)),
        ],
        out_shape=[
            jax.ShapeDtypeStruct((d, t_rows), F32),
            jax.ShapeDtypeStruct((d, t_rows), BF16),
        ],
        compiler_params=_cparams(("parallel",)),
        name="outproj",
    )(x2, merged, w_out, g)


INT_MIN = -2 ** 31


def _sortable(x):
    bits = lax.bitcast_convert_type(x, jnp.int32)
    return bits ^ ((bits >> 31) & 0x7FFFFFFF)


def _unsortable(k):
    return lax.bitcast_convert_type(k ^ ((k >> 31) & 0x7FFFFFFF), F32)


def _top16_ranks(s_ref, shape3, key_ref, val_ref):
    def load_keys():
        key_ref[...] = _sortable(s_ref[...].reshape(shape3) + 0.0)

    def run(extract):
        load_keys()
        lax.fori_loop(0, TOPK, extract, jnp.max(key_ref[...], axis=0))

    def take_all(r, mx):
        val_ref[r] = mx
        cur = key_ref[...]
        cur = jnp.where(cur == mx[None], INT_MIN + 1 + r, cur)
        key_ref[...] = cur
        return jnp.max(cur, axis=0)

    def take_first(r, mx):
        val_ref[r] = mx
        cur = key_ref[...]
        kidx = lax.broadcasted_iota(jnp.int32, shape3, 0)
        first = jnp.min(jnp.where(cur == mx[None], kidx, N_KEYS), axis=0)
        cur = jnp.where(kidx == first[None], INT_MIN + 1 + r, cur)
        key_ref[...] = cur
        return jnp.max(cur, axis=0)

    run(take_all)
    taken = jnp.sum((key_ref[...] <= INT_MIN + TOPK).astype(F32), axis=0)

    @pl.when(jnp.max(taken) > float(TOPK))
    def _():
        run(take_first)

    cur = key_ref[...]
    return jnp.where(cur <= INT_MIN + TOPK, cur - INT_MIN, 0)


def _staircase(a, b):
    ridx = lax.broadcasted_iota(jnp.int32, a.shape, 0)
    top = a[0] + b[0]
    cnt = jnp.zeros(a.shape, jnp.int32)
    nxt = jnp.broadcast_to(b[0][None], a.shape)
    z = jnp.zeros(top.shape, F32)
    for _ in range(TOPK):
        f = a + nxt
        mx = jnp.max(f, axis=0)
        first = jnp.min(jnp.where(f == mx[None], ridx, TOPK), axis=0)
        sel = ridx == first[None]
        cnt = jnp.where(sel, cnt + 1, cnt)
        taken = jnp.max(jnp.where(sel, cnt, 0), axis=0)
        b_next = jnp.full(top.shape, NEG_INF, F32)
        for c in range(1, TOPK):
            b_next = jnp.where(taken == c, b[c], b_next)
        nxt = jnp.where(sel, b_next[None], nxt)
        z = z + jnp.exp(mx - top)
    return cnt, z


def _peer_rank_kernel(xnt_ref, wq_ref, skb_ref, perm_ref, e0_ref, n0_ref, e1_ref, r1_ref,
                      qt_ref, s_ref, key_ref, rank0_ref, a_ref, b_ref):
    tm = xnt_ref.shape[1]
    rows = N_KEYS * PEER_HEADS
    shape3 = (N_KEYS, PEER_HEADS, tm)
    qt_ref[...] = jnp.dot(wq_ref[...], xnt_ref[...], preferred_element_type=F32).astype(BF16)
    for c in range(2):
        s_ref[c] = jnp.dot(skb_ref[c], qt_ref[c * rows:(c + 1) * rows, :], preferred_element_type=F32)

    rank0_ref[...] = _top16_ranks(s_ref.at[0], shape3, key_ref, a_ref)
    rank1 = _top16_ranks(s_ref.at[1], shape3, key_ref, b_ref)
    a = _unsortable(a_ref[...])
    b = _unsortable(b_ref[...])
    cnt, z = _staircase(a, b)

    rank0 = rank0_ref[...]
    n0 = jnp.zeros(shape3, jnp.int32)
    for r in range(TOPK):
        n0 = jnp.where(rank0 == r + 1, cnt[r][None], n0)
    n0_ref[...] = n0.astype(F32)
    e0_ref[...] = jnp.exp(s_ref[0].reshape(shape3) - a[0][None]) * (1.0 / z)[None]

    r1 = jnp.where(rank1 == 0, 2 * TOPK, rank1).astype(F32).astype(BF16).reshape(rows, tm)
    e1 = jnp.exp(s_ref[1].reshape(shape3) - b[0][None]).astype(BF16).reshape(rows, tm)
    out3 = (PEER_HEADS, N_KEYS, tm)
    r1_ref[...] = jnp.dot(perm_ref[...], r1, preferred_element_type=F32).astype(BF16).reshape(out3)
    e1_ref[...] = jnp.dot(perm_ref[...], e1, preferred_element_type=F32).astype(BF16).reshape(out3)


def _peer_rank(xnt, wq_t, skb, perm, tm):
    d, t_rows = xnt.shape
    qd = wq_t.shape[0]
    rows = N_KEYS * PEER_HEADS
    const = dict(pipeline_mode=pl.Buffered(1))
    return pl.pallas_call(
        _peer_rank_kernel,
        grid=(t_rows // tm,),
        in_specs=[
            pl.BlockSpec((d, tm), lambda i: (0, i)),
            pl.BlockSpec((qd, d), lambda i: (0, 0), **const),
            pl.BlockSpec((2, rows, rows), lambda i: (0, 0, 0), **const),
            pl.BlockSpec((rows, rows), lambda i: (0, 0), **const),
        ],
        out_specs=[
            pl.BlockSpec((N_KEYS, PEER_HEADS, tm), lambda i: (0, 0, i)),
            pl.BlockSpec((N_KEYS, PEER_HEADS, tm), lambda i: (0, 0, i)),
            pl.BlockSpec((PEER_HEADS, N_KEYS, tm), lambda i: (0, 0, i)),
            pl.BlockSpec((PEER_HEADS, N_KEYS, tm), lambda i: (0, 0, i)),
        ],
        out_shape=[
            jax.ShapeDtypeStruct((N_KEYS, PEER_HEADS, t_rows), F32),
            jax.ShapeDtypeStruct((N_KEYS, PEER_HEADS, t_rows), F32),
            jax.ShapeDtypeStruct((PEER_HEADS, N_KEYS, t_rows), BF16),
            jax.ShapeDtypeStruct((PEER_HEADS, N_KEYS, t_rows), BF16),
        ],
        scratch_shapes=[
            pltpu.VMEM((qd, tm), BF16),
            pltpu.VMEM((2, rows, tm), F32),
            pltpu.VMEM((N_KEYS, PEER_HEADS, tm), jnp.int32),
            pltpu.VMEM((N_KEYS, PEER_HEADS, tm), jnp.int32),
            pltpu.VMEM((TOPK, PEER_HEADS, tm), jnp.int32),
            pltpu.VMEM((TOPK, PEER_HEADS, tm), jnp.int32),
        ],
        compiler_params=_cparams(("parallel",)),
        name="peer_rank",
    )(xnt, wq_t, skb, perm)


def _peer_mix_kernel(xnt_ref, x1t_ref, u_ref, vt_ref, e0_ref, n0_ref, e1_ref, r1_ref, g_ref,
                     y_ref, acc_ref, a0_ref, a1_ref, *, eb, n_blocks):
    si = pl.program_id(1)
    n_i = eb // N_KEYS
    tm = acc_ref.shape[1]

    @pl.when(si == 0)
    def _():
        acc_ref[...] = jnp.zeros_like(acc_ref)
        a1_ref[...] = jnp.zeros_like(a1_ref)

    def step(cur_ref, prev_ref):
        blk = jnp.maximum(si - 1, 0)
        tiles = []
        for ii in range(n_i):
            i = blk * n_i + ii
            shape = (N_KEYS, tm)
            wgt = jnp.zeros(shape, BF16)
            n_rows = n0_ref[i]
            e_rows = e0_ref[i]
            for h in range(PEER_HEADS):
                n_tile = jnp.broadcast_to(n_rows[h:h + 1], shape).astype(BF16)
                e_tile = jnp.broadcast_to(e_rows[h:h + 1], shape).astype(BF16)
                wgt = wgt + e_tile * jnp.where(r1_ref[h] <= n_tile, e1_ref[h], jnp.zeros(shape, BF16))
            rows = slice(ii * N_KEYS, (ii + 1) * N_KEYS)
            tiles.append(jax.nn.gelu(prev_ref[rows, :].astype(BF16)) * wgt)
        act = jnp.concatenate(tiles, axis=0)
        cur_ref[...] = jnp.dot(u_ref[...], xnt_ref[...], preferred_element_type=F32)
        acc_ref[...] += jnp.dot(vt_ref[...], act, preferred_element_type=F32)

    parity = lax.rem(si, 2)

    @pl.when(parity == 0)
    def _():
        step(a0_ref, a1_ref)

    @pl.when(parity == 1)
    def _():
        step(a1_ref, a0_ref)

    @pl.when(si == n_blocks)
    def _():
        x2 = (x1t_ref[...] + acc_ref[...]).T
        ms = jnp.mean(x2 * x2, axis=-1, keepdims=True)
        y_ref[...] = x2 * lax.rsqrt(ms + EPS) * g_ref[...]


def _peer_mix(xnt, x1t, u, vt, e0, n0, e1, r1, g, tm, eb):
    d, t_rows = xnt.shape
    nb = u.shape[0] // eb
    once = dict(pipeline_mode=pl.Buffered(1))
    head_spec = pl.BlockSpec((PEER_HEADS, N_KEYS, tm), lambda t, s: (0, 0, t))
    key_spec = pl.BlockSpec((N_KEYS, PEER_HEADS, tm), lambda t, s: (0, 0, t), **once)
    return pl.pallas_call(
        functools.partial(_peer_mix_kernel, eb=eb, n_blocks=nb),
        grid=(t_rows // tm, nb + 1),
        in_specs=[
            pl.BlockSpec((d, tm), lambda t, s: (0, t)),
            pl.BlockSpec((d, tm), lambda t, s: (0, t), **once),
            pl.BlockSpec((eb, d), lambda t, s: (jnp.minimum(s, nb - 1), 0)),
            pl.BlockSpec((d, eb), lambda t, s: (0, jnp.maximum(s - 1, 0))),
            key_spec, key_spec, head_spec, head_spec,
            pl.BlockSpec((1, d), lambda t, s: (0, 0)),
        ],
        out_specs=pl.BlockSpec((tm, d), lambda t, s: (t, 0)),
        out_shape=jax.ShapeDtypeStruct((t_rows, d), F32),
        scratch_shapes=[pltpu.VMEM((d, tm), F32), pltpu.VMEM((eb, tm), F32), pltpu.VMEM((eb, tm), F32)],
        compiler_params=_cparams(("parallel", "arbitrary")),
        name="peer_mix",
    )(xnt, x1t, u, vt, e0, n0, e1, r1, g)


def kernel(x, norm_mix, w_in, conv_w, conv_b, ml_gate_bias, da_lambda, da_subln, ml_headnorm,
           w_branch_a, w_branch_b, w_out, norm_ffn, peer_query, peer_subkeys, expert_u, expert_v,
           norm_final):
    batch, seq, d = x.shape
    t_rows = batch * seq
    assert norm_mix.shape[0] == 1, "single-layer trunk"
    assert seq % CHUNK == 0 and d % LANES_V7X == 0
    ts = _tiles(t_rows, seq, d)
    x2 = x.reshape(t_rows, d)

    w = w_in[0]
    gate0 = N_PLAIN_COLS + N_GATE_LOGITS
    w_cat = jnp.concatenate([w[:, :N_PLAIN_COLS], w[:, gate0:]], axis=1).astype(BF16)
    w_if = jnp.pad(w[:, N_PLAIN_COLS:gate0], ((0, 0), (0, LANES_V7X - N_GATE_LOGITS))).astype(BF16)
    gbias = jnp.pad(ml_gate_bias[0], (0, LANES_V7X - N_GATE_LOGITS)).reshape(1, LANES_V7X)
    slopes = 2.0 ** (-8.0 * jnp.arange(1, DA_HEADS + 1, dtype=F32) / DA_HEADS)
    slopes = jnp.broadcast_to(slopes[:, None, None], (DA_HEADS, 1, LANES_V7X))

    proj, gates = _inproj(x2, norm_mix[0].reshape(1, d), w_cat, w_if, ts["proj_tm"], ts["proj_tn"])
    branch_a = _attention(proj, slopes, da_lambda[0], da_subln[0].reshape(HEAD_W, 1), batch, seq, ts["attn_tq"])
    branch_b = _mlstm(proj, gates, gbias, conv_w[0], conv_b[0].reshape(1, -1),
                      ml_headnorm[0].reshape(1, BRANCH_W), batch, seq, ts["ml_len"])
    merged = _merge(branch_a, branch_b, w_branch_a[0].astype(BF16), w_branch_b[0].astype(BF16),
                    proj, d, ts["merge_tm"], ts["merge_tn"])
    x1t, xnt = _outproj(x2, merged, w_out[0].astype(BF16), norm_ffn[0].reshape(1, d), ts["out_tm"])
    half_dim = peer_subkeys.shape[-1]
    wq_t = peer_query[0].reshape(d, PEER_HEADS, 2, half_dim).transpose(2, 1, 3, 0)
    wq_t = wq_t.reshape(2 * PEER_HEADS * half_dim, d).astype(BF16)
    eye = jnp.eye(PEER_HEADS, dtype=peer_subkeys.dtype)
    skb = jnp.einsum("hckd,hg->ckhgd", peer_subkeys[0], eye)
    skb = skb.reshape(2, N_KEYS * PEER_HEADS, PEER_HEADS * half_dim).astype(BF16)
    src_row = jnp.arange(N_KEYS * PEER_HEADS)
    perm = (src_row[None, :] == ((src_row % N_KEYS) * PEER_HEADS + src_row // N_KEYS)[:, None]).astype(BF16)
    e0, n0, e1, r1 = _peer_rank(xnt, wq_t, skb, perm, ts["peer_tm"])
    y = _peer_mix(xnt, x1t, expert_u[0].astype(BF16), expert_v[0].T.astype(BF16), e0, n0, e1, r1,
                  norm_final.reshape(1, d), ts["peer_tm"], ts["peer_eb"])
    return y.reshape(batch, seq, d)
```

```python
import functools

import jax
import jax.numpy as jnp
from jax import lax
from jax.experimental import pallas as pl
from jax.experimental.pallas import tpu as pltpu

F32 = jnp.float32
BF16 = jnp.bfloat16

CHUNK = 64
EPS = 1e-6
LAMBDA_INIT = 0.8 - 0.6
DA_HEADS = 8
DA_QK = 64
HEAD_W = 128
BRANCH_W = DA_HEADS * HEAD_W
ML_HEADS = 8
CONV_W = 4
PEER_HEADS = 8
N_KEYS = 128
TOPK = 16
N_PLAIN_COLS = 7 * BRANCH_W
N_GATE_LOGITS = 2 * ML_HEADS
LOG2E = 1.4426950408889634
QUERY_SCALE = DA_QK ** -0.5 * LOG2E

LANES_V7X = 128
SUBLANES_V7X = 8
MXU_WIDTH_V7X = 256
ATTN_STRIP = MXU_WIDTH_V7X
VMEM_LIMIT_V7X = 56 * 1024 * 1024

NEG_INF = float("-inf")


def _cparams(sem):
    return pltpu.CompilerParams(dimension_semantics=sem, vmem_limit_bytes=VMEM_LIMIT_V7X)


def _tiles(t_rows, seq, d_model):
    return dict(
        proj_tm=min(1024, t_rows), proj_tn=1024 if d_model % 1024 == 0 else 512,
        attn_tq=min(1024, seq),
        ml_len=min(256, seq),
        merge_tm=min(512, t_rows), merge_tn=min(512, d_model),
        out_tm=min(256, t_rows),
        peer_tm=min(512, t_rows), peer_eb=1024,
    )


def _inproj_kernel(x_ref, g_ref, w_ref, wif_ref, proj_ref, if_ref, h_ref, *, n_query, n_plain):
    j = pl.program_id(1)

    @pl.when(j == 0)
    def _():
        x = x_ref[...]
        ms = jnp.mean(x * x, axis=-1, keepdims=True)
        h_ref[...] = (x * lax.rsqrt(ms + EPS) * g_ref[...]).astype(BF16)
        if_ref[...] = jnp.dot(h_ref[...], wif_ref[...], preferred_element_type=F32)

    acc = jnp.dot(h_ref[...], w_ref[...], preferred_element_type=F32)

    @pl.when(j < n_query)
    def _():
        proj_ref[...] = (acc * QUERY_SCALE).astype(BF16)

    @pl.when((j >= n_query) & (j < n_plain))
    def _():
        proj_ref[...] = acc.astype(BF16)

    @pl.when(j >= n_plain)
    def _():
        proj_ref[...] = jax.nn.sigmoid(acc).astype(BF16)


def _inproj(x2, g, w_cat, w_if, tm, tn):
    t_rows, d = x2.shape
    n = w_cat.shape[1]
    return pl.pallas_call(
        functools.partial(_inproj_kernel, n_query=BRANCH_W // tn, n_plain=N_PLAIN_COLS // tn),
        grid=(t_rows // tm, n // tn),
        in_specs=[
            pl.BlockSpec((tm, d), lambda i, j: (i, 0)),
            pl.BlockSpec((1, d), lambda i, j: (0, 0)),
            pl.BlockSpec((d, tn), lambda i, j: (0, j)),
            pl.BlockSpec((d, LANES_V7X), lambda i, j: (0, 0)),
        ],
        out_specs=[
            pl.BlockSpec((tm, tn), lambda i, j: (i, j)),
            pl.BlockSpec((tm, LANES_V7X), lambda i, j: (i, 0)),
        ],
        out_shape=[
            jax.ShapeDtypeStruct((t_rows, n), BF16),
            jax.ShapeDtypeStruct((t_rows, LANES_V7X), F32),
        ],
        scratch_shapes=[pltpu.VMEM((tm, d), BF16)],
        compiler_params=_cparams(("parallel", "arbitrary")),
        name="inproj",
    )(x2, g, w_cat, w_if)


def _attn_kernel(slope_ref, lam_ref, subg_ref, q_ref, k_ref, v_ref, o_ref,
                 kpos_ref, qpos_ref, bdiag_ref, ta_ref, tb_ref, tmax_a_ref, tmax_b_ref,
                 m_ref, l_ref, acc_ref, *, tq):
    qi = pl.program_id(2)
    slope = slope_ref[0:1, 0:1] * LOG2E
    cols = 2 * tq
    split = 16

    @pl.when(qi == 0)
    def _():
        c = lax.broadcasted_iota(jnp.int32, (tq, cols), 0)
        r = lax.broadcasted_iota(jnp.int32, (tq, cols), 1)
        r = jnp.where(r >= tq, r - tq, r)
        allowed = (c // CHUNK) <= (r // CHUNK)
        bdiag_ref[...] = jnp.where(allowed, -slope * jnp.abs(r - c).astype(F32), NEG_INF)
        s_a = slope.astype(BF16).astype(F32)
        s_b = (slope - s_a).astype(BF16).astype(F32)
        s_c = (slope - s_a - s_b).astype(BF16).astype(F32)
        lane = lax.broadcasted_iota(jnp.int32, (tq, HEAD_W), 1)
        off = lax.broadcasted_iota(jnp.int32, (tq, HEAD_W), 0)
        hi = (off // split).astype(F32)
        lo = (off % split).astype(F32)
        piece = jnp.where(lane % 3 == 0, s_a, jnp.where(lane % 3 == 1, s_b, s_c))
        zero = jnp.zeros((tq, HEAD_W), F32)
        kp = jnp.where(lane < 3, hi, jnp.where(lane < 6, lo, jnp.where(lane < 9, split * piece,
                                                                       jnp.where(lane < 12, piece, zero))))
        qp = jnp.where(lane < 3, split * piece, jnp.where(lane < 6, piece, jnp.where(lane < 9, -hi,
                                                                                     jnp.where(lane < 12, -lo, zero))))
        kpos_ref[...] = kp.astype(BF16)
        qpos_ref[0:tq, :] = qp.astype(BF16)
        qpos_ref[tq:cols, :] = qp.astype(BF16)

    q = q_ref[...]
    lane = lax.broadcasted_iota(jnp.int32, (tq, HEAD_W), 1)
    zero = jnp.zeros_like(q)
    q2 = jnp.concatenate([jnp.where(lane < DA_QK, q, zero), jnp.where(lane >= DA_QK, q, zero)], axis=0)
    q2_pos = jnp.concatenate([q2, qpos_ref[...]], axis=1)

    m_ref[...] = jnp.full((1, cols), NEG_INF, F32)
    l_ref[...] = jnp.zeros((1, cols), F32)
    acc_ref[...] = jnp.zeros((HEAD_W, cols), F32)
    nt = (((1,), (1,)), ((), ()))

    def softmax_step(t_ref, tmax_ref, v, shift_const):
        for c0 in range(0, cols, ATTN_STRIP):
            cs = slice(c0, c0 + ATTN_STRIP)
            m_old = m_ref[:, cs]
            m_new = jnp.maximum(m_old, tmax_ref[:, cs] + shift_const)
            p = jnp.exp2(t_ref[:, cs] - (m_new - shift_const))
            alpha = jnp.exp2(m_old - m_new)
            l_ref[:, cs] = alpha * l_ref[:, cs] + jnp.sum(p, axis=0, keepdims=True)
            pv = lax.dot_general(v, p.astype(BF16), (((0,), (0,)), ((), ())), preferred_element_type=F32)
            acc_ref[:, cs] = alpha * acc_ref[:, cs] + pv
            m_ref[:, cs] = m_new

    def scores(entry, t_ref, tmax_ref):
        k0 = pl.multiple_of((entry - 1) * tq, tq)
        k_pos = jnp.concatenate([k_ref[pl.ds(k0, tq), :], kpos_ref[...]], axis=1)
        t = lax.dot_general(k_pos, q2_pos, nt, preferred_element_type=F32)
        t_ref[...] = t
        tmax_ref[...] = jnp.max(t, axis=0, keepdims=True)

    def consume(entry, t_ref, tmax_ref):
        kb = jnp.where(entry == 0, qi, entry - 1)
        k0 = pl.multiple_of(kb * tq, tq)
        softmax_step(t_ref, tmax_ref, v_ref[pl.ds(k0, tq), :], -slope * ((qi - kb) * tq).astype(F32))

    kd = pl.multiple_of(qi * tq, tq)
    s = lax.dot_general(k_ref[pl.ds(kd, tq), :], q2, nt, preferred_element_type=F32) + bdiag_ref[...]
    ta_ref[...] = s
    tmax_a_ref[...] = jnp.max(s, axis=0, keepdims=True)

    def body(i, carry):
        scores(2 * i + 1, tb_ref, tmax_b_ref)
        consume(2 * i, ta_ref, tmax_a_ref)
        scores(2 * i + 2, ta_ref, tmax_a_ref)
        consume(2 * i + 1, tb_ref, tmax_b_ref)
        return carry

    pairs = qi // 2
    lax.fori_loop(0, pairs, body, 0)
    tail = 2 * pairs

    @pl.when(qi % 2 == 1)
    def _():
        scores(tail + 1, tb_ref, tmax_b_ref)
        consume(tail, ta_ref, tmax_a_ref)
        consume(tail + 1, tb_ref, tmax_b_ref)

    @pl.when(qi % 2 == 0)
    def _():
        consume(tail, ta_ref, tmax_a_ref)

    lq = lam_ref[...]
    lam = (jnp.exp(jnp.sum(lq[0:1] * lq[1:2], keepdims=True))
           - jnp.exp(jnp.sum(lq[2:3] * lq[3:4], keepdims=True)) + LAMBDA_INIT)
    o = acc_ref[...] / l_ref[...]
    a = o[:, :tq] - lam * o[:, tq:]
    ms = jnp.mean(a * a, axis=0, keepdims=True)
    y = a * lax.rsqrt(ms + EPS) * subg_ref[...]
    o_ref[...] = (y * (1.0 - LAMBDA_INIT)).T.astype(BF16)


def _attention(proj, slopes, da_lambda, subg, batch, seq, tq):
    t_rows = proj.shape[0]
    nq = seq // tq
    return pl.pallas_call(
        functools.partial(_attn_kernel, tq=tq),
        grid=(batch, DA_HEADS, nq),
        in_specs=[
            pl.BlockSpec((None, 1, LANES_V7X), lambda b, h, i: (h, 0, 0)),
            pl.BlockSpec((4, DA_QK), lambda b, h, i: (0, 0)),
            pl.BlockSpec((HEAD_W, 1), lambda b, h, i: (0, 0)),
            pl.BlockSpec((tq, HEAD_W), lambda b, h, i: (b * nq + i, h)),
            pl.BlockSpec((seq, HEAD_W), lambda b, h, i: (b, DA_HEADS + h)),
            pl.BlockSpec((seq, HEAD_W), lambda b, h, i: (b, 2 * DA_HEADS + h)),
        ],
        out_specs=pl.BlockSpec((tq, HEAD_W), lambda b, h, i: (b * nq + i, h)),
        out_shape=jax.ShapeDtypeStruct((t_rows, BRANCH_W), BF16),
        scratch_shapes=[
            pltpu.VMEM((tq, HEAD_W), BF16),
            pltpu.VMEM((2 * tq, HEAD_W), BF16),
            pltpu.VMEM((tq, 2 * tq), F32),
            pltpu.VMEM((tq, 2 * tq), F32),
            pltpu.VMEM((tq, 2 * tq), F32),
            pltpu.VMEM((1, 2 * tq), F32),
            pltpu.VMEM((1, 2 * tq), F32),
            pltpu.VMEM((1, 2 * tq), F32),
            pltpu.VMEM((1, 2 * tq), F32),
            pltpu.VMEM((HEAD_W, 2 * tq), F32),
        ],
        compiler_params=_cparams(("parallel", "parallel", "arbitrary")),
        name="diff_attention",
    )(slopes, da_lambda, subg, proj, proj, proj)


def _mlstm_kernel(q_ref, k_ref, v_ref, o_ref, gate_ref, gbias_ref, cw_ref, cb_ref, hn_ref,
                  out_ref, c_ref, n_ref, m_ref, qext_ref, kext_ref, *, length):
    ci = pl.program_id(1)
    halo = SUBLANES_V7X

    @pl.when(ci == 0)
    def _():
        c_ref[...] = jnp.zeros_like(c_ref)
        n_ref[...] = jnp.zeros_like(n_ref)
        m_ref[...] = jnp.zeros_like(m_ref)
        qext_ref[0:halo, :] = jnp.zeros((halo, BRANCH_W), F32)
        kext_ref[0:halo, :] = jnp.zeros((halo, BRANCH_W), F32)

    def conv_silu(x_ref, ext_ref, col0):
        ext_ref[halo:, :] = x_ref[...].astype(F32)
        y = cb_ref[:, col0:col0 + BRANCH_W]
        for j in range(CONV_W):
            y = y + cw_ref[j:j + 1, col0:col0 + BRANCH_W] * ext_ref[pl.ds(halo - (CONV_W - 1) + j, length), :]
        ext_ref[0:halo, :] = ext_ref[length:length + halo, :]
        return y * jax.nn.sigmoid(y)

    qc = conv_silu(q_ref, qext_ref, 0).astype(BF16)
    kc = conv_silu(k_ref, kext_ref, BRANCH_W) * (HEAD_W ** -0.5)
    kc_b = kc.astype(BF16)

    g = gate_ref[...] + gbias_ref[...]
    lane = lax.broadcasted_iota(jnp.int32, (length, LANES_V7X), 1)
    lg = jnp.where(lane < ML_HEADS, g, jax.nn.log_sigmoid(g))
    lgt = lg.T
    r = lax.broadcasted_iota(jnp.int32, (length, length), 0)
    c = lax.broadcasted_iota(jnp.int32, (length, length), 1)
    causal = r >= c
    tril = causal.astype(F32)
    triu = (r <= c).astype(F32)
    hi = lax.Precision.HIGHEST
    bcol_all = jnp.dot(tril, lg, precision=hi, preferred_element_type=F32)
    brow_all = jnp.dot(lgt[ML_HEADS:2 * ML_HEADS], triu, precision=hi, preferred_element_type=F32)
    lirow_all = lgt[0:ML_HEADS]

    for h in range(ML_HEADS):
        cs = slice(h * HEAD_W, (h + 1) * HEAD_W)
        b_col = bcol_all[:, ML_HEADS + h:ML_HEADS + h + 1]
        li_col = lg[:, h:h + 1]
        b_row = brow_all[h:h + 1, :]
        li_row = lirow_all[h:h + 1, :]
        m_prev = m_ref[h:h + 1, 0:1]
        n_row = n_ref[h:h + 1, :]
        cmat = c_ref[h]
        qh = qc[:, cs]
        kh = kc[:, cs]
        vh = v_ref[:, cs]

        d = jnp.where(causal, b_col - b_row + li_row, NEG_INF)
        inter = b_col + m_prev
        m_t = jnp.maximum(jnp.max(d, axis=-1, keepdims=True), inter)
        w = jnp.exp(d - m_t)
        sc = jnp.exp(inter - m_t)
        s = lax.dot_general(qh, kc_b[:, cs], (((1,), (1,)), ((), ())), preferred_element_type=F32)
        wqk = w * s
        num = (sc * jnp.dot(qh, cmat.astype(BF16), preferred_element_type=F32)
               + jnp.dot(wqk.astype(BF16), vh, preferred_element_type=F32))
        den = (sc * jnp.sum(qh.astype(F32) * n_row, axis=-1, keepdims=True)
               + jnp.sum(wqk, axis=-1, keepdims=True))
        den = jnp.maximum(jnp.abs(den), jnp.exp(-m_t))
        hh = num / den

        b_last = b_col[length - 1:length, :]
        g_col = b_last - b_col + li_col
        m_new = jnp.maximum(b_last + m_prev, jnp.max(g_col, axis=0, keepdims=True))
        decay = jnp.exp(b_last + m_prev - m_new)
        kw = kh * jnp.exp(g_col - m_new)
        c_ref[h] = decay * cmat + lax.dot_general(
            kw.astype(BF16), vh, (((0,), (0,)), ((), ())), preferred_element_type=F32)
        n_ref[h:h + 1, :] = decay * n_row + jnp.sum(kw, axis=0, keepdims=True)
        m_ref[h:h + 1, :] = jnp.broadcast_to(m_new, (1, LANES_V7X))

        ms = jnp.mean(hh * hh, axis=-1, keepdims=True)
        hn = hh * lax.rsqrt(ms + EPS) * hn_ref[:, cs]
        out_ref[:, cs] = (jax.nn.sigmoid(o_ref[:, cs].astype(F32)) * hn).astype(BF16)


def _mlstm(proj, gates, gbias, conv_w, conv_b, headnorm, batch, seq, length):
    t_rows = proj.shape[0]
    nc = seq // length
    col = lambda k: pl.BlockSpec((length, BRANCH_W), lambda b, c: (b * nc + c, k))
    const = lambda shape: pl.BlockSpec(shape, lambda b, c: (0, 0))
    return pl.pallas_call(
        functools.partial(_mlstm_kernel, length=length),
        grid=(batch, nc),
        in_specs=[
            col(3), col(4), col(5), col(6),
            pl.BlockSpec((length, LANES_V7X), lambda b, c: (b * nc + c, 0)),
            const((1, LANES_V7X)),
            const((CONV_W, 2 * BRANCH_W)),
            const((1, 2 * BRANCH_W)),
            const((1, BRANCH_W)),
        ],
        out_specs=pl.BlockSpec((length, BRANCH_W), lambda b, c: (b * nc + c, 0)),
        out_shape=jax.ShapeDtypeStruct((t_rows, BRANCH_W), BF16),
        scratch_shapes=[
            pltpu.VMEM((ML_HEADS, HEAD_W, HEAD_W), F32),
            pltpu.VMEM((ML_HEADS, HEAD_W), F32),
            pltpu.VMEM((ML_HEADS, LANES_V7X), F32),
            pltpu.VMEM((SUBLANES_V7X + length, BRANCH_W), F32),
            pltpu.VMEM((SUBLANES_V7X + length, BRANCH_W), F32),
        ],
        compiler_params=_cparams(("parallel", "arbitrary")),
        name="mlstm",
    )(proj, proj, proj, proj, gates, gbias, conv_w, conv_b, headnorm)


def _merge_kernel(a_ref, b_ref, wa_ref, wb_ref, ga_ref, gb_ref, out_ref):
    pa = jnp.dot(a_ref[...], wa_ref[...], preferred_element_type=F32)
    pb = jnp.dot(b_ref[...], wb_ref[...], preferred_element_type=F32)
    out_ref[...] = (ga_ref[...].astype(F32) * pa + gb_ref[...].astype(F32) * pb).astype(BF16)


def _merge(branch_a, branch_b, wa, wb, proj, d_model, tm, tn):
    t_rows = branch_a.shape[0]
    ga0 = N_PLAIN_COLS // tn
    gb0 = (N_PLAIN_COLS + d_model) // tn
    return pl.pallas_call(
        _merge_kernel,
        grid=(t_rows // tm, d_model // tn),
        in_specs=[
            pl.BlockSpec((tm, BRANCH_W), lambda i, j: (i, 0)),
            pl.BlockSpec((tm, BRANCH_W), lambda i, j: (i, 0)),
            pl.BlockSpec((BRANCH_W, tn), lambda i, j: (0, j)),
            pl.BlockSpec((BRANCH_W, tn), lambda i, j: (0, j)),
            pl.BlockSpec((tm, tn), lambda i, j: (i, ga0 + j)),
            pl.BlockSpec((tm, tn), lambda i, j: (i, gb0 + j)),
        ],
        out_specs=pl.BlockSpec((tm, tn), lambda i, j: (i, j)),
        out_shape=jax.ShapeDtypeStruct((t_rows, d_model), BF16),
        compiler_params=_cparams(("parallel", "arbitrary")),
        name="merge",
    )(branch_a, branch_b, wa, wb, proj, proj)


def _outproj_kernel(x_ref, mg_ref, w_ref, g_ref, x1t_ref, xnt_ref):
    x1 = x_ref[...] + jnp.dot(mg_ref[...], w_ref[...], preferred_element_type=F32)
    ms = jnp.mean(x1 * x1, axis=-1, keepdims=True)
    xn = x1 * lax.rsqrt(ms + EPS) * g_ref[...]
    x1t_ref[...] = x1.T
    xnt_ref[...] = xn.T.astype(BF16)


def _outproj(x2, merged, w_out, g, tm):
    t_rows, d = x2.shape
    return pl.pallas_call(
        _outproj_kernel,
        grid=(t_rows // tm,),
        in_specs=[
            pl.BlockSpec((tm, d), lambda i: (i, 0)),
            pl.BlockSpec((tm, d), lambda i: (i, 0)),
            pl.BlockSpec((d, d), lambda i: (0, 0)),
            pl.BlockSpec((1, d), lambda i: (0, 0)),
        ],
        out_specs=[
            pl.BlockSpec((d, tm), lambda i: (0, i)),
            pl.BlockSpec((d, tm), lambda i: (0, i)),
        ],
        out_shape=[
            jax.ShapeDtypeStruct((d, t_rows), F32),
            jax.ShapeDtypeStruct((d, t_rows), BF16),
        ],
        compiler_params=_cparams(("parallel",)),
        name="outproj",
    )(x2, merged, w_out, g)


INT_MIN = -2 ** 31


def _sortable(x):
    bits = lax.bitcast_convert_type(x, jnp.int32)
    return bits ^ ((bits >> 31) & 0x7FFFFFFF)


def _unsortable(k):
    return lax.bitcast_convert_type(k ^ ((k >> 31) & 0x7FFFFFFF), F32)


def _top16_ranks(s_ref, shape3, key_ref, val_ref):
    def load_keys():
        key_ref[...] = _sortable(s_ref[...].reshape(shape3) + 0.0)

    def run(extract):
        load_keys()
        lax.fori_loop(0, TOPK, extract, jnp.max(key_ref[...], axis=0))

    def take_all(r, mx):
        val_ref[r] = mx
        cur = key_ref[...]
        cur = jnp.where(cur == mx[None], INT_MIN + 1 + r, cur)
        key_ref[...] = cur
        return jnp.max(cur, axis=0)

    def take_first(r, mx):
        val_ref[r] = mx
        cur = key_ref[...]
        kidx = lax.broadcasted_iota(jnp.int32, shape3, 0)
        first = jnp.min(jnp.where(cur == mx[None], kidx, N_KEYS), axis=0)
        cur = jnp.where(kidx == first[None], INT_MIN + 1 + r, cur)
        key_ref[...] = cur
        return jnp.max(cur, axis=0)

    run(take_all)
    taken = jnp.sum((key_ref[...] <= INT_MIN + TOPK).astype(F32), axis=0)

    @pl.when(jnp.max(taken) > float(TOPK))
    def _():
        run(take_first)

    cur = key_ref[...]
    return jnp.where(cur <= INT_MIN + TOPK, cur - INT_MIN, 0)


def _staircase(a, b):
    ridx = lax.broadcasted_iota(jnp.int32, a.shape, 0)
    top = a[0] + b[0]
    cnt = jnp.zeros(a.shape, jnp.int32)
    nxt = jnp.broadcast_to(b[0][None], a.shape)
    z = jnp.zeros(top.shape, F32)
    for _ in range(TOPK):
        f = a + nxt
        mx = jnp.max(f, axis=0)
        first = jnp.min(jnp.where(f == mx[None], ridx, TOPK), axis=0)
        sel = ridx == first[None]
        cnt = jnp.where(sel, cnt + 1, cnt)
        taken = jnp.max(jnp.where(sel, cnt, 0), axis=0)
        b_next = jnp.full(top.shape, NEG_INF, F32)
        for c in range(1, TOPK):
            b_next = jnp.where(taken == c, b[c], b_next)
        nxt = jnp.where(sel, b_next[None], nxt)
        z = z + jnp.exp(mx - top)
    return cnt, z


def _peer_rank_kernel(xnt_ref, wq_ref, skb_ref, perm_ref, e0_ref, n0_ref, e1_ref, r1_ref,
                      qt_ref, s_ref, key_ref, rank0_ref, a_ref, b_ref):
    tm = xnt_ref.shape[1]
    rows = N_KEYS * PEER_HEADS
    shape3 = (N_KEYS, PEER_HEADS, tm)
    qt_ref[...] = jnp.dot(wq_ref[...], xnt_ref[...], preferred_element_type=F32).astype(BF16)
    for c in range(2):
        s_ref[c] = jnp.dot(skb_ref[c], qt_ref[c * rows:(c + 1) * rows, :], preferred_element_type=F32)

    rank0_ref[...] = _top16_ranks(s_ref.at[0], shape3, key_ref, a_ref)
    rank1 = _top16_ranks(s_ref.at[1], shape3, key_ref, b_ref)
    a = _unsortable(a_ref[...])
    b = _unsortable(b_ref[...])
    cnt, z = _staircase(a, b)

    rank0 = rank0_ref[...]
    n0 = jnp.zeros(shape3, jnp.int32)
    for r in range(TOPK):
        n0 = jnp.where(rank0 == r + 1, cnt[r][None], n0)
    n0_ref[...] = n0.astype(F32)
    e0_ref[...] = jnp.exp(s_ref[0].reshape(shape3) - a[0][None]) * (1.0 / z)[None]

    r1 = jnp.where(rank1 == 0, 2 * TOPK, rank1).astype(F32).astype(BF16).reshape(rows, tm)
    e1 = jnp.exp(s_ref[1].reshape(shape3) - b[0][None]).astype(BF16).reshape(rows, tm)
    out3 = (PEER_HEADS, N_KEYS, tm)
    r1_ref[...] = jnp.dot(perm_ref[...], r1, preferred_element_type=F32).astype(BF16).reshape(out3)
    e1_ref[...] = jnp.dot(perm_ref[...], e1, preferred_element_type=F32).astype(BF16).reshape(out3)


def _peer_rank(xnt, wq_t, skb, perm, tm):
    d, t_rows = xnt.shape
    qd = wq_t.shape[0]
    rows = N_KEYS * PEER_HEADS
    const = dict(pipeline_mode=pl.Buffered(1))
    return pl.pallas_call(
        _peer_rank_kernel,
        grid=(t_rows // tm,),
        in_specs=[
            pl.BlockSpec((d, tm), lambda i: (0, i)),
            pl.BlockSpec((qd, d), lambda i: (0, 0), **const),
            pl.BlockSpec((2, rows, rows), lambda i: (0, 0, 0), **const),
            pl.BlockSpec((rows, rows), lambda i: (0, 0), **const),
        ],
        out_specs=[
            pl.BlockSpec((N_KEYS, PEER_HEADS, tm), lambda i: (0, 0, i)),
            pl.BlockSpec((N_KEYS, PEER_HEADS, tm), lambda i: (0, 0, i)),
            pl.BlockSpec((PEER_HEADS, N_KEYS, tm), lambda i: (0, 0, i)),
            pl.BlockSpec((PEER_HEADS, N_KEYS, tm), lambda i: (0, 0, i)),
        ],
        out_shape=[
            jax.ShapeDtypeStruct((N_KEYS, PEER_HEADS, t_rows), F32),
            jax.ShapeDtypeStruct((N_KEYS, PEER_HEADS, t_rows), F32),
            jax.ShapeDtypeStruct((PEER_HEADS, N_KEYS, t_rows), BF16),
            jax.ShapeDtypeStruct((PEER_HEADS, N_KEYS, t_rows), BF16),
        ],
        scratch_shapes=[
            pltpu.VMEM((qd, tm), BF16),
            pltpu.VMEM((2, rows, tm), F32),
            pltpu.VMEM((N_KEYS, PEER_HEADS, tm), jnp.int32),
            pltpu.VMEM((N_KEYS, PEER_HEADS, tm), jnp.int32),
            pltpu.VMEM((TOPK, PEER_HEADS, tm), jnp.int32),
            pltpu.VMEM((TOPK, PEER_HEADS, tm), jnp.int32),
        ],
        compiler_params=_cparams(("parallel",)),
        name="peer_rank",
    )(xnt, wq_t, skb, perm)


def _peer_mix_kernel(xnt_ref, x1t_ref, u_ref, vt_ref, e0_ref, n0_ref, e1_ref, r1_ref, g_ref,
                     y_ref, acc_ref, a0_ref, a1_ref, *, eb, n_blocks):
    si = pl.program_id(1)
    n_i = eb // N_KEYS
    tm = acc_ref.shape[1]

    @pl.when(si == 0)
    def _():
        acc_ref[...] = jnp.zeros_like(acc_ref)
        a1_ref[...] = jnp.zeros_like(a1_ref)

    def step(cur_ref, prev_ref):
        blk = jnp.maximum(si - 1, 0)
        tiles = []
        for ii in range(n_i):
            i = blk * n_i + ii
            shape = (N_KEYS, tm)
            wgt = jnp.zeros(shape, BF16)
            n_rows = n0_ref[i]
            e_rows = e0_ref[i]
            for h in range(PEER_HEADS):
                n_tile = jnp.broadcast_to(n_rows[h:h + 1], shape).astype(BF16)
                e_tile = jnp.broadcast_to(e_rows[h:h + 1], shape).astype(BF16)
                wgt = wgt + e_tile * jnp.where(r1_ref[h] <= n_tile, e1_ref[h], jnp.zeros(shape, BF16))
            rows = slice(ii * N_KEYS, (ii + 1) * N_KEYS)
            tiles.append(jax.nn.gelu(prev_ref[rows, :].astype(BF16)) * wgt)
        act = jnp.concatenate(tiles, axis=0)
        cur_ref[...] = jnp.dot(u_ref[...], xnt_ref[...], preferred_element_type=F32)
        acc_ref[...] += jnp.dot(vt_ref[...], act, preferred_element_type=F32)

    parity = lax.rem(si, 2)

    @pl.when(parity == 0)
    def _():
        step(a0_ref, a1_ref)

    @pl.when(parity == 1)
    def _():
        step(a1_ref, a0_ref)

    @pl.when(si == n_blocks)
    def _():
        x2 = (x1t_ref[...] + acc_ref[...]).T
        ms = jnp.mean(x2 * x2, axis=-1, keepdims=True)
        y_ref[...] = x2 * lax.rsqrt(ms + EPS) * g_ref[...]


def _peer_mix(xnt, x1t, u, vt, e0, n0, e1, r1, g, tm, eb):
    d, t_rows = xnt.shape
    nb = u.shape[0] // eb
    once = dict(pipeline_mode=pl.Buffered(1))
    head_spec = pl.BlockSpec((PEER_HEADS, N_KEYS, tm), lambda t, s: (0, 0, t))
    key_spec = pl.BlockSpec((N_KEYS, PEER_HEADS, tm), lambda t, s: (0, 0, t), **once)
    return pl.pallas_call(
        functools.partial(_peer_mix_kernel, eb=eb, n_blocks=nb),
        grid=(t_rows // tm, nb + 1),
        in_specs=[
            pl.BlockSpec((d, tm), lambda t, s: (0, t)),
            pl.BlockSpec((d, tm), lambda t, s: (0, t), **once),
            pl.BlockSpec((eb, d), lambda t, s: (jnp.minimum(s, nb - 1), 0)),
            pl.BlockSpec((d, eb), lambda t, s: (0, jnp.maximum(s - 1, 0))),
            key_spec, key_spec, head_spec, head_spec,
            pl.BlockSpec((1, d), lambda t, s: (0, 0)),
        ],
        out_specs=pl.BlockSpec((tm, d), lambda t, s: (t, 0)),
        out_shape=jax.ShapeDtypeStruct((t_rows, d), F32),
        scratch_shapes=[pltpu.VMEM((d, tm), F32), pltpu.VMEM((eb, tm), F32), pltpu.VMEM((eb, tm), F32)],
        compiler_params=_cparams(("parallel", "arbitrary")),
        name="peer_mix",
    )(xnt, x1t, u, vt, e0, n0, e1, r1, g)


def kernel(x, norm_mix, w_in, conv_w, conv_b, ml_gate_bias, da_lambda, da_subln, ml_headnorm,
           w_branch_a, w_branch_b, w_out, norm_ffn, peer_query, peer_subkeys, expert_u, expert_v,
           norm_final):
    batch, seq, d = x.shape
    t_rows = batch * seq
    assert norm_mix.shape[0] == 1, "single-layer trunk"
    assert seq % CHUNK == 0 and d % LANES_V7X == 0
    ts = _tiles(t_rows, seq, d)
    x2 = x.reshape(t_rows, d)

    w = w_in[0]
    gate0 = N_PLAIN_COLS + N_GATE_LOGITS
    w_cat = jnp.concatenate([w[:, :N_PLAIN_COLS], w[:, gate0:]], axis=1).astype(BF16)
    w_if = jnp.pad(w[:, N_PLAIN_COLS:gate0], ((0, 0), (0, LANES_V7X - N_GATE_LOGITS))).astype(BF16)
    gbias = jnp.pad(ml_gate_bias[0], (0, LANES_V7X - N_GATE_LOGITS)).reshape(1, LANES_V7X)
    slopes = 2.0 ** (-8.0 * jnp.arange(1, DA_HEADS + 1, dtype=F32) / DA_HEADS)
    slopes = jnp.broadcast_to(slopes[:, None, None], (DA_HEADS, 1, LANES_V7X))

    proj, gates = _inproj(x2, norm_mix[0].reshape(1, d), w_cat, w_if, ts["proj_tm"], ts["proj_tn"])
    branch_a = _attention(proj, slopes, da_lambda[0], da_subln[0].reshape(HEAD_W, 1), batch, seq, ts["attn_tq"])
    branch_b = _mlstm(proj, gates, gbias, conv_w[0], conv_b[0].reshape(1, -1),
                      ml_headnorm[0].reshape(1, BRANCH_W), batch, seq, ts["ml_len"])
    merged = _merge(branch_a, branch_b, w_branch_a[0].astype(BF16), w_branch_b[0].astype(BF16),
                    proj, d, ts["merge_tm"], ts["merge_tn"])
    x1t, xnt = _outproj(x2, merged, w_out[0].astype(BF16), norm_ffn[0].reshape(1, d), ts["out_tm"])
    half_dim = peer_subkeys.shape[-1]
    wq_t = peer_query[0].reshape(d, PEER_HEADS, 2, half_dim).transpose(2, 1, 3, 0)
    wq_t = wq_t.reshape(2 * PEER_HEADS * half_dim, d).astype(BF16)
    eye = jnp.eye(PEER_HEADS, dtype=peer_subkeys.dtype)
    skb = jnp.einsum("hckd,hg->ckhgd", peer_subkeys[0], eye)
    skb = skb.reshape(2, N_KEYS * PEER_HEADS, PEER_HEADS * half_dim).astype(BF16)
    src_row = jnp.arange(N_KEYS * PEER_HEADS)
    perm = (src_row[None, :] == ((src_row % N_KEYS) * PEER_HEADS + src_row // N_KEYS)[:, None]).astype(BF16)
    e0, n0, e1, r1 = _peer_rank(xnt, wq_t, skb, perm, ts["peer_tm"])
    y = _peer_mix(xnt, x1t, expert_u[0].astype(BF16), expert_v[0].T.astype(BF16), e0, n0, e1, r1,
                  norm_final.reshape(1, d), ts["peer_tm"], ts["peer_eb"])
    return y.reshape(batch, seq, d)
```

```python
import functools

import jax
import jax.numpy as jnp
from jax import lax
from jax.experimental import pallas as pl
from jax.experimental.pallas import tpu as pltpu

F32 = jnp.float32
BF16 = jnp.bfloat16

CHUNK = 64
EPS = 1e-6
LAMBDA_INIT = 0.8 - 0.6
DA_HEADS = 8
DA_QK = 64
HEAD_W = 128
BRANCH_W = DA_HEADS * HEAD_W
ML_HEADS = 8
CONV_W = 4
PEER_HEADS = 8
N_KEYS = 128
TOPK = 16
N_PLAIN_COLS = 7 * BRANCH_W
N_GATE_LOGITS = 2 * ML_HEADS
LOG2E = 1.4426950408889634
QUERY_SCALE = DA_QK ** -0.5 * LOG2E

LANES_V7X = 128
SUBLANES_V7X = 8
MXU_WIDTH_V7X = 256
ATTN_STRIP = MXU_WIDTH_V7X
VMEM_LIMIT_V7X = 56 * 1024 * 1024

NEG_INF = float("-inf")


def _cparams(sem):
    return pltpu.CompilerParams(dimension_semantics=sem, vmem_limit_bytes=VMEM_LIMIT_V7X)


def _tiles(t_rows, seq, d_model):
    return dict(
        proj_tm=min(1024, t_rows), proj_tn=1024 if d_model % 1024 == 0 else 512,
        attn_tq=min(1024, seq),
        ml_len=min(256, seq),
        merge_tm=min(1024, t_rows), merge_tn=min(1024, d_model),
        out_tm=min(512, t_rows),
        merge_out_tm=min(256, t_rows),
        peer_tm=min(512, t_rows), peer_eb=1024,
    )


def _inproj_kernel(x_ref, g_ref, w_ref, wif_ref, proj_ref, if_ref, h_ref, *, n_query, n_plain):
    j = pl.program_id(1)

    @pl.when(j == 0)
    def _():
        x = x_ref[...]
        ms = jnp.mean(x * x, axis=-1, keepdims=True)
        h_ref[...] = (x * lax.rsqrt(ms + EPS) * g_ref[...]).astype(BF16)
        if_ref[...] = jnp.dot(h_ref[...], wif_ref[...], preferred_element_type=F32)

    acc = jnp.dot(h_ref[...], w_ref[...], preferred_element_type=F32)

    @pl.when(j < n_query)
    def _():
        proj_ref[...] = (acc * QUERY_SCALE).astype(BF16)

    @pl.when((j >= n_query) & (j < n_plain))
    def _():
        proj_ref[...] = acc.astype(BF16)

    @pl.when(j >= n_plain)
    def _():
        proj_ref[...] = jax.nn.sigmoid(acc).astype(BF16)


def _inproj(x2, g, w_cat, w_if, tm, tn):
    t_rows, d = x2.shape
    n = w_cat.shape[1]
    return pl.pallas_call(
        functools.partial(_inproj_kernel, n_query=BRANCH_W // tn, n_plain=N_PLAIN_COLS // tn),
        grid=(t_rows // tm, n // tn),
        in_specs=[
            pl.BlockSpec((tm, d), lambda i, j: (i, 0)),
            pl.BlockSpec((1, d), lambda i, j: (0, 0)),
            pl.BlockSpec((d, tn), lambda i, j: (0, j)),
            pl.BlockSpec((d, LANES_V7X), lambda i, j: (0, 0)),
        ],
        out_specs=[
            pl.BlockSpec((tm, tn), lambda i, j: (i, j)),
            pl.BlockSpec((tm, LANES_V7X), lambda i, j: (i, 0)),
        ],
        out_shape=[
            jax.ShapeDtypeStruct((t_rows, n), BF16),
            jax.ShapeDtypeStruct((t_rows, LANES_V7X), F32),
        ],
        scratch_shapes=[pltpu.VMEM((tm, d), BF16)],
        compiler_params=_cparams(("parallel", "arbitrary")),
        name="inproj",
    )(x2, g, w_cat, w_if)


def _attn_kernel(slope_ref, lam_ref, subg_ref, q_ref, k_ref, v_ref, o_ref,
                 kpos_ref, qpos_ref, bdiag_ref, ta_ref, tb_ref, tmax_a_ref, tmax_b_ref,
                 m_ref, l_ref, acc_ref, *, tq):
    qi = pl.program_id(2)
    slope = slope_ref[0:1, 0:1] * LOG2E
    cols = 2 * tq
    split = 16

    @pl.when(qi == 0)
    def _():
        c = lax.broadcasted_iota(jnp.int32, (tq, cols), 0)
        r = lax.broadcasted_iota(jnp.int32, (tq, cols), 1)
        r = jnp.where(r >= tq, r - tq, r)
        allowed = (c // CHUNK) <= (r // CHUNK)
        bdiag_ref[...] = jnp.where(allowed, -slope * jnp.abs(r - c).astype(F32), NEG_INF)
        s_a = slope.astype(BF16).astype(F32)
        s_b = (slope - s_a).astype(BF16).astype(F32)
        s_c = (slope - s_a - s_b).astype(BF16).astype(F32)
        lane = lax.broadcasted_iota(jnp.int32, (tq, HEAD_W), 1)
        off = lax.broadcasted_iota(jnp.int32, (tq, HEAD_W), 0)
        hi = (off // split).astype(F32)
        lo = (off % split).astype(F32)
        piece = jnp.where(lane % 3 == 0, s_a, jnp.where(lane % 3 == 1, s_b, s_c))
        zero = jnp.zeros((tq, HEAD_W), F32)
        kp = jnp.where(lane < 3, hi, jnp.where(lane < 6, lo, jnp.where(lane < 9, split * piece,
                                                                       jnp.where(lane < 12, piece, zero))))
        qp = jnp.where(lane < 3, split * piece, jnp.where(lane < 6, piece, jnp.where(lane < 9, -hi,
                                                                                     jnp.where(lane < 12, -lo, zero))))
        kpos_ref[...] = kp.astype(BF16)
        qpos_ref[0:tq, :] = qp.astype(BF16)
        qpos_ref[tq:cols, :] = qp.astype(BF16)

    q = q_ref[...]
    lane = lax.broadcasted_iota(jnp.int32, (tq, HEAD_W), 1)
    zero = jnp.zeros_like(q)
    q2 = jnp.concatenate([jnp.where(lane < DA_QK, q, zero), jnp.where(lane >= DA_QK, q, zero)], axis=0)
    q2_pos = jnp.concatenate([q2, qpos_ref[...]], axis=1)

    m_ref[...] = jnp.full((1, cols), NEG_INF, F32)
    l_ref[...] = jnp.zeros((1, cols), F32)
    acc_ref[...] = jnp.zeros((HEAD_W, cols), F32)
    nt = (((1,), (1,)), ((), ()))

    def softmax_step(t_ref, tmax_ref, v, shift_const):
        for c0 in range(0, cols, ATTN_STRIP):
            cs = slice(c0, c0 + ATTN_STRIP)
            m_old = m_ref[:, cs]
            m_new = jnp.maximum(m_old, tmax_ref[:, cs] + shift_const)
            p = jnp.exp2(t_ref[:, cs] - (m_new - shift_const))
            alpha = jnp.exp2(m_old - m_new)
            l_ref[:, cs] = alpha * l_ref[:, cs] + jnp.sum(p, axis=0, keepdims=True)
            pv = lax.dot_general(v, p.astype(BF16), (((0,), (0,)), ((), ())), preferred_element_type=F32)
            acc_ref[:, cs] = alpha * acc_ref[:, cs] + pv
            m_ref[:, cs] = m_new

    def scores(entry, t_ref, tmax_ref):
        k0 = pl.multiple_of((entry - 1) * tq, tq)
        k_pos = jnp.concatenate([k_ref[pl.ds(k0, tq), :], kpos_ref[...]], axis=1)
        t = lax.dot_general(k_pos, q2_pos, nt, preferred_element_type=F32)
        t_ref[...] = t
        tmax_ref[...] = jnp.max(t, axis=0, keepdims=True)

    def consume(entry, t_ref, tmax_ref):
        kb = jnp.where(entry == 0, qi, entry - 1)
        k0 = pl.multiple_of(kb * tq, tq)
        softmax_step(t_ref, tmax_ref, v_ref[pl.ds(k0, tq), :], -slope * ((qi - kb) * tq).astype(F32))

    kd = pl.multiple_of(qi * tq, tq)
    s = lax.dot_general(k_ref[pl.ds(kd, tq), :], q2, nt, preferred_element_type=F32) + bdiag_ref[...]
    ta_ref[...] = s
    tmax_a_ref[...] = jnp.max(s, axis=0, keepdims=True)

    def body(i, carry):
        scores(2 * i + 1, tb_ref, tmax_b_ref)
        consume(2 * i, ta_ref, tmax_a_ref)
        scores(2 * i + 2, ta_ref, tmax_a_ref)
        consume(2 * i + 1, tb_ref, tmax_b_ref)
        return carry

    pairs = qi // 2
    lax.fori_loop(0, pairs, body, 0)
    tail = 2 * pairs

    @pl.when(qi % 2 == 1)
    def _():
        scores(tail + 1, tb_ref, tmax_b_ref)
        consume(tail, ta_ref, tmax_a_ref)
        consume(tail + 1, tb_ref, tmax_b_ref)

    @pl.when(qi % 2 == 0)
    def _():
        consume(tail, ta_ref, tmax_a_ref)

    lq = lam_ref[...]
    lam = (jnp.exp(jnp.sum(lq[0:1] * lq[1:2], keepdims=True))
           - jnp.exp(jnp.sum(lq[2:3] * lq[3:4], keepdims=True)) + LAMBDA_INIT)
    o = acc_ref[...] / l_ref[...]
    a = o[:, :tq] - lam * o[:, tq:]
    ms = jnp.mean(a * a, axis=0, keepdims=True)
    y = a * lax.rsqrt(ms + EPS) * subg_ref[...]
    o_ref[...] = (y * (1.0 - LAMBDA_INIT)).T.astype(BF16)


def _attention(proj, slopes, da_lambda, subg, batch, seq, tq):
    t_rows = proj.shape[0]
    nq = seq // tq
    return pl.pallas_call(
        functools.partial(_attn_kernel, tq=tq),
        grid=(batch, DA_HEADS, nq),
        in_specs=[
            pl.BlockSpec((None, 1, LANES_V7X), lambda b, h, i: (h, 0, 0)),
            pl.BlockSpec((4, DA_QK), lambda b, h, i: (0, 0)),
            pl.BlockSpec((HEAD_W, 1), lambda b, h, i: (0, 0)),
            pl.BlockSpec((tq, HEAD_W), lambda b, h, i: (b * nq + i, h)),
            pl.BlockSpec((seq, HEAD_W), lambda b, h, i: (b, DA_HEADS + h)),
            pl.BlockSpec((seq, HEAD_W), lambda b, h, i: (b, 2 * DA_HEADS + h)),
        ],
        out_specs=pl.BlockSpec((tq, HEAD_W), lambda b, h, i: (b * nq + i, h)),
        out_shape=jax.ShapeDtypeStruct((t_rows, BRANCH_W), BF16),
        scratch_shapes=[
            pltpu.VMEM((tq, HEAD_W), BF16),
            pltpu.VMEM((2 * tq, HEAD_W), BF16),
            pltpu.VMEM((tq, 2 * tq), F32),
            pltpu.VMEM((tq, 2 * tq), F32),
            pltpu.VMEM((tq, 2 * tq), F32),
            pltpu.VMEM((1, 2 * tq), F32),
            pltpu.VMEM((1, 2 * tq), F32),
            pltpu.VMEM((1, 2 * tq), F32),
            pltpu.VMEM((1, 2 * tq), F32),
            pltpu.VMEM((HEAD_W, 2 * tq), F32),
        ],
        compiler_params=_cparams(("parallel", "parallel", "arbitrary")),
        name="diff_attention",
    )(slopes, da_lambda, subg, proj, proj, proj)


def _mlstm_kernel(q_ref, k_ref, v_ref, o_ref, gate_ref, gbias_ref, cw_ref, cb_ref, hn_ref,
                  out_ref, c_ref, n_ref, m_ref, qext_ref, kext_ref, *, length):
    ci = pl.program_id(1)
    halo = SUBLANES_V7X

    @pl.when(ci == 0)
    def _():
        c_ref[...] = jnp.zeros_like(c_ref)
        n_ref[...] = jnp.zeros_like(n_ref)
        m_ref[...] = jnp.zeros_like(m_ref)
        qext_ref[0:halo, :] = jnp.zeros((halo, BRANCH_W), F32)
        kext_ref[0:halo, :] = jnp.zeros((halo, BRANCH_W), F32)

    def conv_silu(x_ref, ext_ref, col0):
        ext_ref[halo:, :] = x_ref[...].astype(F32)
        y = cb_ref[:, col0:col0 + BRANCH_W]
        for j in range(CONV_W):
            y = y + cw_ref[j:j + 1, col0:col0 + BRANCH_W] * ext_ref[pl.ds(halo - (CONV_W - 1) + j, length), :]
        ext_ref[0:halo, :] = ext_ref[length:length + halo, :]
        return y * jax.nn.sigmoid(y)

    qc = conv_silu(q_ref, qext_ref, 0).astype(BF16)
    kc = conv_silu(k_ref, kext_ref, BRANCH_W) * (HEAD_W ** -0.5)
    kc_b = kc.astype(BF16)

    g = gate_ref[...] + gbias_ref[...]
    lane = lax.broadcasted_iota(jnp.int32, (length, LANES_V7X), 1)
    lg = jnp.where(lane < ML_HEADS, g, jax.nn.log_sigmoid(g))
    lgt = lg.T
    r = lax.broadcasted_iota(jnp.int32, (length, length), 0)
    c = lax.broadcasted_iota(jnp.int32, (length, length), 1)
    causal = r >= c
    tril = causal.astype(F32)
    triu = (r <= c).astype(F32)
    hi = lax.Precision.HIGHEST
    bcol_all = jnp.dot(tril, lg, precision=hi, preferred_element_type=F32)
    brow_all = jnp.dot(lgt[ML_HEADS:2 * ML_HEADS], triu, precision=hi, preferred_element_type=F32)
    lirow_all = lgt[0:ML_HEADS]

    for h in range(ML_HEADS):
        cs = slice(h * HEAD_W, (h + 1) * HEAD_W)
        b_col = bcol_all[:, ML_HEADS + h:ML_HEADS + h + 1]
        li_col = lg[:, h:h + 1]
        b_row = brow_all[h:h + 1, :]
        li_row = lirow_all[h:h + 1, :]
        m_prev = m_ref[h:h + 1, 0:1]
        n_row = n_ref[h:h + 1, :]
        cmat = c_ref[h]
        qh = qc[:, cs]
        kh = kc[:, cs]
        vh = v_ref[:, cs]

        d = jnp.where(causal, b_col - b_row + li_row, NEG_INF)
        inter = b_col + m_prev
        m_t = jnp.maximum(jnp.max(d, axis=-1, keepdims=True), inter)
        w = jnp.exp(d - m_t)
        sc = jnp.exp(inter - m_t)
        s = lax.dot_general(qh, kc_b[:, cs], (((1,), (1,)), ((), ())), preferred_element_type=F32)
        wqk = w * s
        num = (sc * jnp.dot(qh, cmat.astype(BF16), preferred_element_type=F32)
               + jnp.dot(wqk.astype(BF16), vh, preferred_element_type=F32))
        den = (sc * jnp.sum(qh.astype(F32) * n_row, axis=-1, keepdims=True)
               + jnp.sum(wqk, axis=-1, keepdims=True))
        den = jnp.maximum(jnp.abs(den), jnp.exp(-m_t))
        hh = num / den

        b_last = b_col[length - 1:length, :]
        g_col = b_last - b_col + li_col
        m_new = jnp.maximum(b_last + m_prev, jnp.max(g_col, axis=0, keepdims=True))
        decay = jnp.exp(b_last + m_prev - m_new)
        kw = kh * jnp.exp(g_col - m_new)
        c_ref[h] = decay * cmat + lax.dot_general(
            kw.astype(BF16), vh, (((0,), (0,)), ((), ())), preferred_element_type=F32)
        n_ref[h:h + 1, :] = decay * n_row + jnp.sum(kw, axis=0, keepdims=True)
        m_ref[h:h + 1, :] = jnp.broadcast_to(m_new, (1, LANES_V7X))

        ms = jnp.mean(hh * hh, axis=-1, keepdims=True)
        hn = hh * lax.rsqrt(ms + EPS) * hn_ref[:, cs]
        out_ref[:, cs] = (jax.nn.sigmoid(o_ref[:, cs].astype(F32)) * hn).astype(BF16)


def _mlstm(proj, gates, gbias, conv_w, conv_b, headnorm, batch, seq, length):
    t_rows = proj.shape[0]
    nc = seq // length
    col = lambda k: pl.BlockSpec((length, BRANCH_W), lambda b, c: (b * nc + c, k))
    const = lambda shape: pl.BlockSpec(shape, lambda b, c: (0, 0))
    return pl.pallas_call(
        functools.partial(_mlstm_kernel, length=length),
        grid=(batch, nc),
        in_specs=[
            col(3), col(4), col(5), col(6),
            pl.BlockSpec((length, LANES_V7X), lambda b, c: (b * nc + c, 0)),
            const((1, LANES_V7X)),
            const((CONV_W, 2 * BRANCH_W)),
            const((1, 2 * BRANCH_W)),
            const((1, BRANCH_W)),
        ],
        out_specs=pl.BlockSpec((length, BRANCH_W), lambda b, c: (b * nc + c, 0)),
        out_shape=jax.ShapeDtypeStruct((t_rows, BRANCH_W), BF16),
        scratch_shapes=[
            pltpu.VMEM((ML_HEADS, HEAD_W, HEAD_W), F32),
            pltpu.VMEM((ML_HEADS, HEAD_W), F32),
            pltpu.VMEM((ML_HEADS, LANES_V7X), F32),
            pltpu.VMEM((SUBLANES_V7X + length, BRANCH_W), F32),
            pltpu.VMEM((SUBLANES_V7X + length, BRANCH_W), F32),
        ],
        compiler_params=_cparams(("parallel", "arbitrary")),
        name="mlstm",
    )(proj, proj, proj, proj, gates, gbias, conv_w, conv_b, headnorm)


def _merge_kernel(a_ref, b_ref, wa_ref, wb_ref, ga_ref, gb_ref, out_ref):
    pa = jnp.dot(a_ref[...], wa_ref[...], preferred_element_type=F32)
    pb = jnp.dot(b_ref[...], wb_ref[...], preferred_element_type=F32)
    out_ref[...] = (ga_ref[...].astype(F32) * pa + gb_ref[...].astype(F32) * pb).astype(BF16)


def _merge(branch_a, branch_b, wa, wb, proj, d_model, tm, tn):
    t_rows = branch_a.shape[0]
    ga0 = N_PLAIN_COLS // tn
    gb0 = (N_PLAIN_COLS + d_model) // tn
    return pl.pallas_call(
        _merge_kernel,
        grid=(t_rows // tm, d_model // tn),
        in_specs=[
            pl.BlockSpec((tm, BRANCH_W), lambda i, j: (i, 0)),
            pl.BlockSpec((tm, BRANCH_W), lambda i, j: (i, 0)),
            pl.BlockSpec((BRANCH_W, tn), lambda i, j: (0, j)),
            pl.BlockSpec((BRANCH_W, tn), lambda i, j: (0, j)),
            pl.BlockSpec((tm, tn), lambda i, j: (i, ga0 + j)),
            pl.BlockSpec((tm, tn), lambda i, j: (i, gb0 + j)),
        ],
        out_specs=pl.BlockSpec((tm, tn), lambda i, j: (i, j)),
        out_shape=jax.ShapeDtypeStruct((t_rows, d_model), BF16),
        compiler_params=_cparams(("parallel", "arbitrary")),
        name="merge",
    )(branch_a, branch_b, wa, wb, proj, proj)


def _outproj_kernel(x_ref, mg_ref, w_ref, g_ref, x1t_ref, xnt_ref):
    x1 = x_ref[...] + jnp.dot(mg_ref[...], w_ref[...], preferred_element_type=F32)
    ms = jnp.mean(x1 * x1, axis=-1, keepdims=True)
    xn = x1 * lax.rsqrt(ms + EPS) * g_ref[...]
    x1t_ref[...] = x1.T
    xnt_ref[...] = xn.T.astype(BF16)


def _outproj(x2, merged, w_out, g, tm):
    t_rows, d = x2.shape
    return pl.pallas_call(
        _outproj_kernel,
        grid=(t_rows // tm,),
        in_specs=[
            pl.BlockSpec((tm, d), lambda i: (i, 0)),
            pl.BlockSpec((tm, d), lambda i: (i, 0)),
            pl.BlockSpec((d, d), lambda i: (0, 0)),
            pl.BlockSpec((1, d), lambda i: (0, 0)),
        ],
        out_specs=[
            pl.BlockSpec((d, tm), lambda i: (0, i)),
            pl.BlockSpec((d, tm), lambda i: (0, i)),
        ],
        out_shape=[
            jax.ShapeDtypeStruct((d, t_rows), F32),
            jax.ShapeDtypeStruct((d, t_rows), BF16),
        ],
        compiler_params=_cparams(("parallel",)),
        name="outproj",
    )(x2, merged, w_out, g)


def _merge_out_kernel(*refs, n_gate):
    a_ref, b_ref, wa_ref, wb_ref = refs[:4]
    ga_refs = refs[4:4 + n_gate]
    gb_refs = refs[4 + n_gate:4 + 2 * n_gate]
    x_ref, w_ref, g_ref, x1t_ref, xnt_ref = refs[4 + 2 * n_gate:]
    pa = jnp.dot(a_ref[...], wa_ref[...], preferred_element_type=F32)
    pb = jnp.dot(b_ref[...], wb_ref[...], preferred_element_type=F32)
    ga = jnp.concatenate([r[...] for r in ga_refs], axis=1).astype(F32)
    gb = jnp.concatenate([r[...] for r in gb_refs], axis=1).astype(F32)
    merged = (ga * pa + gb * pb).astype(BF16)
    x1 = x_ref[...] + jnp.dot(merged, w_ref[...], preferred_element_type=F32)
    ms = jnp.mean(x1 * x1, axis=-1, keepdims=True)
    xn = x1 * lax.rsqrt(ms + EPS) * g_ref[...]
    x1t_ref[...] = x1.T
    xnt_ref[...] = xn.T.astype(BF16)


def _merge_out(branch_a, branch_b, wa, wb, proj, x2, w_out, g, tm):
    t_rows, d = x2.shape
    gw = min(BRANCH_W, d)
    n_gate = d // gw
    ga0 = N_PLAIN_COLS // gw
    gb0 = (N_PLAIN_COLS + d) // gw
    once = dict(pipeline_mode=pl.Buffered(1))
    gate_specs = ([pl.BlockSpec((tm, gw), functools.partial(lambda i, c: (i, c), c=ga0 + k)) for k in range(n_gate)]
                  + [pl.BlockSpec((tm, gw), functools.partial(lambda i, c: (i, c), c=gb0 + k)) for k in range(n_gate)])
    return pl.pallas_call(
        functools.partial(_merge_out_kernel, n_gate=n_gate),
        grid=(t_rows // tm,),
        in_specs=[
            pl.BlockSpec((tm, BRANCH_W), lambda i: (i, 0)),
            pl.BlockSpec((tm, BRANCH_W), lambda i: (i, 0)),
            pl.BlockSpec((BRANCH_W, d), lambda i: (0, 0), **once),
            pl.BlockSpec((BRANCH_W, d), lambda i: (0, 0), **once),
            *gate_specs,
            pl.BlockSpec((tm, d), lambda i: (i, 0)),
            pl.BlockSpec((d, d), lambda i: (0, 0), **once),
            pl.BlockSpec((1, d), lambda i: (0, 0)),
        ],
        out_specs=[
            pl.BlockSpec((d, tm), lambda i: (0, i)),
            pl.BlockSpec((d, tm), lambda i: (0, i)),
        ],
        out_shape=[
            jax.ShapeDtypeStruct((d, t_rows), F32),
            jax.ShapeDtypeStruct((d, t_rows), BF16),
        ],
        compiler_params=_cparams(("parallel",)),
        name="merge_out",
    )(branch_a, branch_b, wa, wb, *([proj] * (2 * n_gate)), x2, w_out, g)


INT_MIN = -2 ** 31


def _sortable(x):
    bits = lax.bitcast_convert_type(x, jnp.int32)
    return bits ^ ((bits >> 31) & 0x7FFFFFFF)


def _unsortable(k):
    return lax.bitcast_convert_type(k ^ ((k >> 31) & 0x7FFFFFFF), F32)


def _top16_ranks(s_ref, shape3, key_ref, val_ref):
    def load_keys():
        key_ref[...] = _sortable(s_ref[...].reshape(shape3) + 0.0)

    def run(extract):
        load_keys()
        lax.fori_loop(0, TOPK, extract, jnp.max(key_ref[...], axis=0))

    def take_all(r, mx):
        val_ref[r] = mx
        cur = key_ref[...]
        cur = jnp.where(cur == mx[None], INT_MIN + 1 + r, cur)
        key_ref[...] = cur
        return jnp.max(cur, axis=0)

    def take_first(r, mx):
        val_ref[r] = mx
        cur = key_ref[...]
        kidx = lax.broadcasted_iota(jnp.int32, shape3, 0)
        first = jnp.min(jnp.where(cur == mx[None], kidx, N_KEYS), axis=0)
        cur = jnp.where(kidx == first[None], INT_MIN + 1 + r, cur)
        key_ref[...] = cur
        return jnp.max(cur, axis=0)

    run(take_all)
    taken = jnp.sum((key_ref[...] <= INT_MIN + TOPK).astype(F32), axis=0)

    @pl.when(jnp.max(taken) > float(TOPK))
    def _():
        run(take_first)

    cur = key_ref[...]
    return jnp.where(cur <= INT_MIN + TOPK, cur - INT_MIN, 0)


def _staircase(a, b):
    ridx = lax.broadcasted_iota(jnp.int32, a.shape, 0)
    top = a[0] + b[0]
    cnt = jnp.zeros(a.shape, jnp.int32)
    nxt = jnp.broadcast_to(b[0][None], a.shape)
    z = jnp.zeros(top.shape, F32)
    for _ in range(TOPK):
        f = a + nxt
        mx = jnp.max(f, axis=0)
        first = jnp.min(jnp.where(f == mx[None], ridx, TOPK), axis=0)
        sel = ridx == first[None]
        cnt = jnp.where(sel, cnt + 1, cnt)
        taken = jnp.max(jnp.where(sel, cnt, 0), axis=0)
        b_next = jnp.full(top.shape, NEG_INF, F32)
        for c in range(1, TOPK):
            b_next = jnp.where(taken == c, b[c], b_next)
        nxt = jnp.where(sel, b_next[None], nxt)
        z = z + jnp.exp(mx - top)
    return cnt, z


def _peer_rank_kernel(xnt_ref, wq_ref, skb_ref, perm_ref, e0_ref, n0_ref, e1_ref, r1_ref,
                      qt_ref, s_ref, key_ref, rank0_ref, a_ref, b_ref):
    tm = xnt_ref.shape[1]
    rows = N_KEYS * PEER_HEADS
    shape3 = (N_KEYS, PEER_HEADS, tm)
    qt_ref[...] = jnp.dot(wq_ref[...], xnt_ref[...], preferred_element_type=F32).astype(BF16)
    for c in range(2):
        s_ref[c] = jnp.dot(skb_ref[c], qt_ref[c * rows:(c + 1) * rows, :], preferred_element_type=F32)

    rank0_ref[...] = _top16_ranks(s_ref.at[0], shape3, key_ref, a_ref)
    rank1 = _top16_ranks(s_ref.at[1], shape3, key_ref, b_ref)
    a = _unsortable(a_ref[...])
    b = _unsortable(b_ref[...])
    cnt, z = _staircase(a, b)

    rank0 = rank0_ref[...]
    n0 = jnp.zeros(shape3, jnp.int32)
    for r in range(TOPK):
        n0 = jnp.where(rank0 == r + 1, cnt[r][None], n0)
    n0_ref[...] = n0.astype(F32)
    e0_ref[...] = jnp.exp(s_ref[0].reshape(shape3) - a[0][None]) * (1.0 / z)[None]

    r1 = jnp.where(rank1 == 0, 2 * TOPK, rank1).astype(F32).astype(BF16).reshape(rows, tm)
    e1 = jnp.exp(s_ref[1].reshape(shape3) - b[0][None]).astype(BF16).reshape(rows, tm)
    out3 = (PEER_HEADS, N_KEYS, tm)
    r1_ref[...] = jnp.dot(perm_ref[...], r1, preferred_element_type=F32).astype(BF16).reshape(out3)
    e1_ref[...] = jnp.dot(perm_ref[...], e1, preferred_element_type=F32).astype(BF16).reshape(out3)


def _peer_rank(xnt, wq_t, skb, perm, tm):
    d, t_rows = xnt.shape
    qd = wq_t.shape[0]
    rows = N_KEYS * PEER_HEADS
    const = dict(pipeline_mode=pl.Buffered(1))
    return pl.pallas_call(
        _peer_rank_kernel,
        grid=(t_rows // tm,),
        in_specs=[
            pl.BlockSpec((d, tm), lambda i: (0, i)),
            pl.BlockSpec((qd, d), lambda i: (0, 0), **const),
            pl.BlockSpec((2, rows, rows), lambda i: (0, 0, 0), **const),
            pl.BlockSpec((rows, rows), lambda i: (0, 0), **const),
        ],
        out_specs=[
            pl.BlockSpec((N_KEYS, PEER_HEADS, tm), lambda i: (0, 0, i)),
            pl.BlockSpec((N_KEYS, PEER_HEADS, tm), lambda i: (0, 0, i)),
            pl.BlockSpec((PEER_HEADS, N_KEYS, tm), lambda i: (0, 0, i)),
            pl.BlockSpec((PEER_HEADS, N_KEYS, tm), lambda i: (0, 0, i)),
        ],
        out_shape=[
            jax.ShapeDtypeStruct((N_KEYS, PEER_HEADS, t_rows), F32),
            jax.ShapeDtypeStruct((N_KEYS, PEER_HEADS, t_rows), F32),
            jax.ShapeDtypeStruct((PEER_HEADS, N_KEYS, t_rows), BF16),
            jax.ShapeDtypeStruct((PEER_HEADS, N_KEYS, t_rows), BF16),
        ],
        scratch_shapes=[
            pltpu.VMEM((qd, tm), BF16),
            pltpu.VMEM((2, rows, tm), F32),
            pltpu.VMEM((N_KEYS, PEER_HEADS, tm), jnp.int32),
            pltpu.VMEM((N_KEYS, PEER_HEADS, tm), jnp.int32),
            pltpu.VMEM((TOPK, PEER_HEADS, tm), jnp.int32),
            pltpu.VMEM((TOPK, PEER_HEADS, tm), jnp.int32),
        ],
        compiler_params=_cparams(("parallel",)),
        name="peer_rank",
    )(xnt, wq_t, skb, perm)


def _peer_mix_kernel(xnt_ref, x1t_ref, u_ref, vt_ref, e0_ref, n0_ref, e1_ref, r1_ref, g_ref,
                     y_ref, acc_ref, a0_ref, a1_ref, *, eb, n_blocks):
    si = pl.program_id(1)
    n_i = eb // N_KEYS
    tm = acc_ref.shape[1]

    @pl.when(si == 0)
    def _():
        acc_ref[...] = jnp.zeros_like(acc_ref)
        a1_ref[...] = jnp.zeros_like(a1_ref)

    def step(cur_ref, prev_ref):
        blk = jnp.maximum(si - 1, 0)
        tiles = []
        for ii in range(n_i):
            i = blk * n_i + ii
            shape = (N_KEYS, tm)
            wgt = jnp.zeros(shape, BF16)
            n_rows = n0_ref[i]
            e_rows = e0_ref[i]
            for h in range(PEER_HEADS):
                n_tile = jnp.broadcast_to(n_rows[h:h + 1], shape).astype(BF16)
                e_tile = jnp.broadcast_to(e_rows[h:h + 1], shape).astype(BF16)
                wgt = wgt + e_tile * jnp.where(r1_ref[h] <= n_tile, e1_ref[h], jnp.zeros(shape, BF16))
            rows = slice(ii * N_KEYS, (ii + 1) * N_KEYS)
            tiles.append(jax.nn.gelu(prev_ref[rows, :].astype(BF16)) * wgt)
        act = jnp.concatenate(tiles, axis=0)
        cur_ref[...] = jnp.dot(u_ref[...], xnt_ref[...], preferred_element_type=F32)
        acc_ref[...] += jnp.dot(vt_ref[...], act, preferred_element_type=F32)

    parity = lax.rem(si, 2)

    @pl.when(parity == 0)
    def _():
        step(a0_ref, a1_ref)

    @pl.when(parity == 1)
    def _():
        step(a1_ref, a0_ref)

    @pl.when(si == n_blocks)
    def _():
        x2 = (x1t_ref[...] + acc_ref[...]).T
        ms = jnp.mean(x2 * x2, axis=-1, keepdims=True)
        y_ref[...] = x2 * lax.rsqrt(ms + EPS) * g_ref[...]


def _peer_mix(xnt, x1t, u, vt, e0, n0, e1, r1, g, tm, eb):
    d, t_rows = xnt.shape
    nb = u.shape[0] // eb
    once = dict(pipeline_mode=pl.Buffered(1))
    head_spec = pl.BlockSpec((PEER_HEADS, N_KEYS, tm), lambda t, s: (0, 0, t))
    key_spec = pl.BlockSpec((N_KEYS, PEER_HEADS, tm), lambda t, s: (0, 0, t), **once)
    return pl.pallas_call(
        functools.partial(_peer_mix_kernel, eb=eb, n_blocks=nb),
        grid=(t_rows // tm, nb + 1),
        in_specs=[
            pl.BlockSpec((d, tm), lambda t, s: (0, t)),
            pl.BlockSpec((d, tm), lambda t, s: (0, t), **once),
            pl.BlockSpec((eb, d), lambda t, s: (jnp.minimum(s, nb - 1), 0)),
            pl.BlockSpec((d, eb), lambda t, s: (0, jnp.maximum(s - 1, 0))),
            key_spec, key_spec, head_spec, head_spec,
            pl.BlockSpec((1, d), lambda t, s: (0, 0)),
        ],
        out_specs=pl.BlockSpec((tm, d), lambda t, s: (t, 0)),
        out_shape=jax.ShapeDtypeStruct((t_rows, d), F32),
        scratch_shapes=[pltpu.VMEM((d, tm), F32), pltpu.VMEM((eb, tm), F32), pltpu.VMEM((eb, tm), F32)],
        compiler_params=_cparams(("parallel", "arbitrary")),
        name="peer_mix",
    )(xnt, x1t, u, vt, e0, n0, e1, r1, g)


def kernel(x, norm_mix, w_in, conv_w, conv_b, ml_gate_bias, da_lambda, da_subln, ml_headnorm,
           w_branch_a, w_branch_b, w_out, norm_ffn, peer_query, peer_subkeys, expert_u, expert_v,
           norm_final):
    batch, seq, d = x.shape
    t_rows = batch * seq
    assert norm_mix.shape[0] == 1, "single-layer trunk"
    assert seq % CHUNK == 0 and d % LANES_V7X == 0
    ts = _tiles(t_rows, seq, d)
    x2 = x.reshape(t_rows, d)

    w = w_in[0]
    gate0 = N_PLAIN_COLS + N_GATE_LOGITS
    w_cat = jnp.concatenate([w[:, :N_PLAIN_COLS], w[:, gate0:]], axis=1).astype(BF16)
    w_if = jnp.pad(w[:, N_PLAIN_COLS:gate0], ((0, 0), (0, LANES_V7X - N_GATE_LOGITS))).astype(BF16)
    gbias = jnp.pad(ml_gate_bias[0], (0, LANES_V7X - N_GATE_LOGITS)).reshape(1, LANES_V7X)
    slopes = 2.0 ** (-8.0 * jnp.arange(1, DA_HEADS + 1, dtype=F32) / DA_HEADS)
    slopes = jnp.broadcast_to(slopes[:, None, None], (DA_HEADS, 1, LANES_V7X))

    proj, gates = _inproj(x2, norm_mix[0].reshape(1, d), w_cat, w_if, ts["proj_tm"], ts["proj_tn"])
    branch_a = _attention(proj, slopes, da_lambda[0], da_subln[0].reshape(HEAD_W, 1), batch, seq, ts["attn_tq"])
    branch_b = _mlstm(proj, gates, gbias, conv_w[0], conv_b[0].reshape(1, -1),
                      ml_headnorm[0].reshape(1, BRANCH_W), batch, seq, ts["ml_len"])
    x1t, xnt = _merge_out(branch_a, branch_b, w_branch_a[0].astype(BF16), w_branch_b[0].astype(BF16), proj,
                          x2, w_out[0].astype(BF16), norm_ffn[0].reshape(1, d), ts["merge_out_tm"])
    half_dim = peer_subkeys.shape[-1]
    wq_t = peer_query[0].reshape(d, PEER_HEADS, 2, half_dim).transpose(2, 1, 3, 0)
    wq_t = wq_t.reshape(2 * PEER_HEADS * half_dim, d).astype(BF16)
    eye = jnp.eye(PEER_HEADS, dtype=peer_subkeys.dtype)
    skb = jnp.einsum("hckd,hg->ckhgd", peer_subkeys[0], eye)
    skb = skb.reshape(2, N_KEYS * PEER_HEADS, PEER_HEADS * half_dim).astype(BF16)
    src_row = jnp.arange(N_KEYS * PEER_HEADS)
    perm = (src_row[None, :] == ((src_row % N_KEYS) * PEER_HEADS + src_row // N_KEYS)[:, None]).astype(BF16)
    e0, n0, e1, r1 = _peer_rank(xnt, wq_t, skb, perm, ts["peer_tm"])
    y = _peer_mix(xnt, x1t, expert_u[0].astype(BF16), expert_v[0].T.astype(BF16), e0, n0, e1, r1,
                  norm_final.reshape(1, d), ts["peer_tm"], ts["peer_eb"])
    return y.reshape(batch, seq, d)
```

```python
import functools

import jax
import jax.numpy as jnp
from jax import lax
from jax.experimental import pallas as pl
from jax.experimental.pallas import tpu as pltpu

F32 = jnp.float32
BF16 = jnp.bfloat16

CHUNK = 64
EPS = 1e-6
LAMBDA_INIT = 0.8 - 0.6
DA_HEADS = 8
DA_QK = 64
HEAD_W = 128
BRANCH_W = DA_HEADS * HEAD_W
ML_HEADS = 8
CONV_W = 4
PEER_HEADS = 8
N_KEYS = 128
TOPK = 16
N_PLAIN_COLS = 7 * BRANCH_W
N_GATE_LOGITS = 2 * ML_HEADS
LOG2E = 1.4426950408889634
QUERY_SCALE = DA_QK ** -0.5 * LOG2E

LANES_V7X = 128
SUBLANES_V7X = 8
MXU_WIDTH_V7X = 256
ATTN_STRIP = MXU_WIDTH_V7X
VMEM_LIMIT_V7X = 56 * 1024 * 1024

NEG_INF = float("-inf")


def _cparams(sem):
    return pltpu.CompilerParams(dimension_semantics=sem, vmem_limit_bytes=VMEM_LIMIT_V7X)


def _tiles(t_rows, seq, d_model):
    return dict(
        proj_tm=min(1024, t_rows), proj_tn=1024 if d_model % 1024 == 0 else 512,
        attn_tq=min(1024, seq),
        ml_len=min(256, seq),
        merge_out_tm=min(256, t_rows),
        peer_tm=min(512, t_rows), peer_eb=1024,
    )


def _inproj_kernel(x_ref, g_ref, w_ref, wif_ref, proj_ref, if_ref, h_ref, *, n_query, n_plain):
    j = pl.program_id(1)

    @pl.when(j == 0)
    def _():
        x = x_ref[...]
        ms = jnp.mean(x * x, axis=-1, keepdims=True)
        h_ref[...] = (x * lax.rsqrt(ms + EPS) * g_ref[...]).astype(BF16)
        if_ref[...] = jnp.dot(h_ref[...], wif_ref[...], preferred_element_type=F32)

    acc = jnp.dot(h_ref[...], w_ref[...], preferred_element_type=F32)

    @pl.when(j < n_query)
    def _():
        proj_ref[...] = (acc * QUERY_SCALE).astype(BF16)

    @pl.when((j >= n_query) & (j < n_plain))
    def _():
        proj_ref[...] = acc.astype(BF16)

    @pl.when(j >= n_plain)
    def _():
        proj_ref[...] = jax.nn.sigmoid(acc).astype(BF16)


def _inproj(x2, g, w_cat, w_if, tm, tn):
    t_rows, d = x2.shape
    n = w_cat.shape[1]
    return pl.pallas_call(
        functools.partial(_inproj_kernel, n_query=BRANCH_W // tn, n_plain=N_PLAIN_COLS // tn),
        grid=(t_rows // tm, n // tn),
        in_specs=[
            pl.BlockSpec((tm, d), lambda i, j: (i, 0)),
            pl.BlockSpec((1, d), lambda i, j: (0, 0)),
            pl.BlockSpec((d, tn), lambda i, j: (0, j)),
            pl.BlockSpec((d, LANES_V7X), lambda i, j: (0, 0)),
        ],
        out_specs=[
            pl.BlockSpec((tm, tn), lambda i, j: (i, j)),
            pl.BlockSpec((tm, LANES_V7X), lambda i, j: (i, 0)),
        ],
        out_shape=[
            jax.ShapeDtypeStruct((t_rows, n), BF16),
            jax.ShapeDtypeStruct((t_rows, LANES_V7X), F32),
        ],
        scratch_shapes=[pltpu.VMEM((tm, d), BF16)],
        compiler_params=_cparams(("parallel", "arbitrary")),
        name="inproj",
    )(x2, g, w_cat, w_if)


def _attn_kernel(slope_ref, lam_ref, subg_ref, q_ref, k_ref, v_ref, o_ref,
                 kpos_ref, qpos_ref, bdiag_ref, ta_ref, tb_ref, tmax_a_ref, tmax_b_ref,
                 m_ref, l_ref, acc_ref, *, tq):
    qi = pl.program_id(2)
    slope = slope_ref[0:1, 0:1] * LOG2E
    cols = 2 * tq
    split = 16

    @pl.when(qi == 0)
    def _():
        c = lax.broadcasted_iota(jnp.int32, (tq, cols), 0)
        r = lax.broadcasted_iota(jnp.int32, (tq, cols), 1)
        r = jnp.where(r >= tq, r - tq, r)
        allowed = (c // CHUNK) <= (r // CHUNK)
        bdiag_ref[...] = jnp.where(allowed, -slope * jnp.abs(r - c).astype(F32), NEG_INF)
        s_a = slope.astype(BF16).astype(F32)
        s_b = (slope - s_a).astype(BF16).astype(F32)
        s_c = (slope - s_a - s_b).astype(BF16).astype(F32)
        lane = lax.broadcasted_iota(jnp.int32, (tq, HEAD_W), 1)
        off = lax.broadcasted_iota(jnp.int32, (tq, HEAD_W), 0)
        hi = (off // split).astype(F32)
        lo = (off % split).astype(F32)
        piece = jnp.where(lane % 3 == 0, s_a, jnp.where(lane % 3 == 1, s_b, s_c))
        zero = jnp.zeros((tq, HEAD_W), F32)
        kp = jnp.where(lane < 3, hi, jnp.where(lane < 6, lo, jnp.where(lane < 9, split * piece,
                                                                       jnp.where(lane < 12, piece, zero))))
        qp = jnp.where(lane < 3, split * piece, jnp.where(lane < 6, piece, jnp.where(lane < 9, -hi,
                                                                                     jnp.where(lane < 12, -lo, zero))))
        kpos_ref[...] = kp.astype(BF16)
        qpos_ref[0:tq, :] = qp.astype(BF16)
        qpos_ref[tq:cols, :] = qp.astype(BF16)

    q = q_ref[...]
    lane = lax.broadcasted_iota(jnp.int32, (tq, HEAD_W), 1)
    zero = jnp.zeros_like(q)
    q2 = jnp.concatenate([jnp.where(lane < DA_QK, q, zero), jnp.where(lane >= DA_QK, q, zero)], axis=0)
    q2_pos = jnp.concatenate([q2, qpos_ref[...]], axis=1)

    m_ref[...] = jnp.full((1, cols), NEG_INF, F32)
    l_ref[...] = jnp.zeros((1, cols), F32)
    acc_ref[...] = jnp.zeros((HEAD_W, cols), F32)
    nt = (((1,), (1,)), ((), ()))

    def softmax_step(t_ref, tmax_ref, v, shift_const):
        for c0 in range(0, cols, ATTN_STRIP):
            cs = slice(c0, c0 + ATTN_STRIP)
            m_old = m_ref[:, cs]
            m_new = jnp.maximum(m_old, tmax_ref[:, cs] + shift_const)
            p = jnp.exp2(t_ref[:, cs] - (m_new - shift_const))
            alpha = jnp.exp2(m_old - m_new)
            l_ref[:, cs] = alpha * l_ref[:, cs] + jnp.sum(p, axis=0, keepdims=True)
            pv = lax.dot_general(v, p.astype(BF16), (((0,), (0,)), ((), ())), preferred_element_type=F32)
            acc_ref[:, cs] = alpha * acc_ref[:, cs] + pv
            m_ref[:, cs] = m_new

    def scores(entry, t_ref, tmax_ref):
        k0 = pl.multiple_of((entry - 1) * tq, tq)
        k_pos = jnp.concatenate([k_ref[pl.ds(k0, tq), :], kpos_ref[...]], axis=1)
        t = lax.dot_general(k_pos, q2_pos, nt, preferred_element_type=F32)
        t_ref[...] = t
        tmax_ref[...] = jnp.max(t, axis=0, keepdims=True)

    def consume(entry, t_ref, tmax_ref):
        kb = jnp.where(entry == 0, qi, entry - 1)
        k0 = pl.multiple_of(kb * tq, tq)
        softmax_step(t_ref, tmax_ref, v_ref[pl.ds(k0, tq), :], -slope * ((qi - kb) * tq).astype(F32))

    kd = pl.multiple_of(qi * tq, tq)
    s = lax.dot_general(k_ref[pl.ds(kd, tq), :], q2, nt, preferred_element_type=F32) + bdiag_ref[...]
    ta_ref[...] = s
    tmax_a_ref[...] = jnp.max(s, axis=0, keepdims=True)

    def body(i, carry):
        scores(2 * i + 1, tb_ref, tmax_b_ref)
        consume(2 * i, ta_ref, tmax_a_ref)
        scores(2 * i + 2, ta_ref, tmax_a_ref)
        consume(2 * i + 1, tb_ref, tmax_b_ref)
        return carry

    pairs = qi // 2
    lax.fori_loop(0, pairs, body, 0)
    tail = 2 * pairs

    @pl.when(qi % 2 == 1)
    def _():
        scores(tail + 1, tb_ref, tmax_b_ref)
        consume(tail, ta_ref, tmax_a_ref)
        consume(tail + 1, tb_ref, tmax_b_ref)

    @pl.when(qi % 2 == 0)
    def _():
        consume(tail, ta_ref, tmax_a_ref)

    lq = lam_ref[...]
    lam = (jnp.exp(jnp.sum(lq[0:1] * lq[1:2], keepdims=True))
           - jnp.exp(jnp.sum(lq[2:3] * lq[3:4], keepdims=True)) + LAMBDA_INIT)
    o = acc_ref[...] / l_ref[...]
    a = o[:, :tq] - lam * o[:, tq:]
    ms = jnp.mean(a * a, axis=0, keepdims=True)
    y = a * lax.rsqrt(ms + EPS) * subg_ref[...]
    o_ref[...] = (y * (1.0 - LAMBDA_INIT)).T.astype(BF16)


def _attention(proj, slopes, da_lambda, subg, batch, seq, tq):
    t_rows = proj.shape[0]
    nq = seq // tq
    return pl.pallas_call(
        functools.partial(_attn_kernel, tq=tq),
        grid=(batch, DA_HEADS, nq),
        in_specs=[
            pl.BlockSpec((None, 1, LANES_V7X), lambda b, h, i: (h, 0, 0)),
            pl.BlockSpec((4, DA_QK), lambda b, h, i: (0, 0)),
            pl.BlockSpec((HEAD_W, 1), lambda b, h, i: (0, 0)),
            pl.BlockSpec((tq, HEAD_W), lambda b, h, i: (b * nq + i, h)),
            pl.BlockSpec((seq, HEAD_W), lambda b, h, i: (b, DA_HEADS + h)),
            pl.BlockSpec((seq, HEAD_W), lambda b, h, i: (b, 2 * DA_HEADS + h)),
        ],
        out_specs=pl.BlockSpec((tq, HEAD_W), lambda b, h, i: (b * nq + i, h)),
        out_shape=jax.ShapeDtypeStruct((t_rows, BRANCH_W), BF16),
        scratch_shapes=[
            pltpu.VMEM((tq, HEAD_W), BF16),
            pltpu.VMEM((2 * tq, HEAD_W), BF16),
            pltpu.VMEM((tq, 2 * tq), F32),
            pltpu.VMEM((tq, 2 * tq), F32),
            pltpu.VMEM((tq, 2 * tq), F32),
            pltpu.VMEM((1, 2 * tq), F32),
            pltpu.VMEM((1, 2 * tq), F32),
            pltpu.VMEM((1, 2 * tq), F32),
            pltpu.VMEM((1, 2 * tq), F32),
            pltpu.VMEM((HEAD_W, 2 * tq), F32),
        ],
        compiler_params=_cparams(("parallel", "parallel", "arbitrary")),
        name="diff_attention",
    )(slopes, da_lambda, subg, proj, proj, proj)


def _mlstm_kernel(q_ref, k_ref, v_ref, o_ref, gate_ref, gbias_ref, cw_ref, cb_ref, hn_ref,
                  out_ref, c_ref, n_ref, m_ref, qext_ref, kext_ref, *, length):
    ci = pl.program_id(1)
    halo = SUBLANES_V7X

    @pl.when(ci == 0)
    def _():
        c_ref[...] = jnp.zeros_like(c_ref)
        n_ref[...] = jnp.zeros_like(n_ref)
        m_ref[...] = jnp.zeros_like(m_ref)
        qext_ref[0:halo, :] = jnp.zeros((halo, BRANCH_W), F32)
        kext_ref[0:halo, :] = jnp.zeros((halo, BRANCH_W), F32)

    def conv_silu(x_ref, ext_ref, col0):
        ext_ref[halo:, :] = x_ref[...].astype(F32)
        y = cb_ref[:, col0:col0 + BRANCH_W]
        for j in range(CONV_W):
            y = y + cw_ref[j:j + 1, col0:col0 + BRANCH_W] * ext_ref[pl.ds(halo - (CONV_W - 1) + j, length), :]
        ext_ref[0:halo, :] = ext_ref[length:length + halo, :]
        return y * jax.nn.sigmoid(y)

    qc = conv_silu(q_ref, qext_ref, 0).astype(BF16)
    kc = conv_silu(k_ref, kext_ref, BRANCH_W) * (HEAD_W ** -0.5)
    kc_b = kc.astype(BF16)

    g = gate_ref[...] + gbias_ref[...]
    lane = lax.broadcasted_iota(jnp.int32, (length, LANES_V7X), 1)
    lg = jnp.where(lane < ML_HEADS, g, jax.nn.log_sigmoid(g))
    lgt = lg.T
    r = lax.broadcasted_iota(jnp.int32, (length, length), 0)
    c = lax.broadcasted_iota(jnp.int32, (length, length), 1)
    causal = r >= c
    tril = causal.astype(F32)
    triu = (r <= c).astype(F32)
    hi = lax.Precision.HIGHEST
    bcol_all = jnp.dot(tril, lg, precision=hi, preferred_element_type=F32)
    brow_all = jnp.dot(lgt[ML_HEADS:2 * ML_HEADS], triu, precision=hi, preferred_element_type=F32)
    lirow_all = lgt[0:ML_HEADS]

    for h in range(ML_HEADS):
        cs = slice(h * HEAD_W, (h + 1) * HEAD_W)
        b_col = bcol_all[:, ML_HEADS + h:ML_HEADS + h + 1]
        li_col = lg[:, h:h + 1]
        b_row = brow_all[h:h + 1, :]
        li_row = lirow_all[h:h + 1, :]
        m_prev = m_ref[h:h + 1, 0:1]
        n_row = n_ref[h:h + 1, :]
        cmat = c_ref[h]
        qh = qc[:, cs]
        kh = kc[:, cs]
        vh = v_ref[:, cs]

        d = jnp.where(causal, b_col - b_row + li_row, NEG_INF)
        inter = b_col + m_prev
        m_t = jnp.maximum(jnp.max(d, axis=-1, keepdims=True), inter)
        w = jnp.exp(d - m_t)
        sc = jnp.exp(inter - m_t)
        s = lax.dot_general(qh, kc_b[:, cs], (((1,), (1,)), ((), ())), preferred_element_type=F32)
        wqk = w * s
        num = (sc * jnp.dot(qh, cmat.astype(BF16), preferred_element_type=F32)
               + jnp.dot(wqk.astype(BF16), vh, preferred_element_type=F32))
        den = (sc * jnp.sum(qh.astype(F32) * n_row, axis=-1, keepdims=True)
               + jnp.sum(wqk, axis=-1, keepdims=True))
        den = jnp.maximum(jnp.abs(den), jnp.exp(-m_t))
        hh = num / den

        b_last = b_col[length - 1:length, :]
        g_col = b_last - b_col + li_col
        m_new = jnp.maximum(b_last + m_prev, jnp.max(g_col, axis=0, keepdims=True))
        decay = jnp.exp(b_last + m_prev - m_new)
        kw = kh * jnp.exp(g_col - m_new)
        c_ref[h] = decay * cmat + lax.dot_general(
            kw.astype(BF16), vh, (((0,), (0,)), ((), ())), preferred_element_type=F32)
        n_ref[h:h + 1, :] = decay * n_row + jnp.sum(kw, axis=0, keepdims=True)
        m_ref[h:h + 1, :] = jnp.broadcast_to(m_new, (1, LANES_V7X))

        ms = jnp.mean(hh * hh, axis=-1, keepdims=True)
        hn = hh * lax.rsqrt(ms + EPS) * hn_ref[:, cs]
        out_ref[:, cs] = (jax.nn.sigmoid(o_ref[:, cs].astype(F32)) * hn).astype(BF16)


def _mlstm(proj, gates, gbias, conv_w, conv_b, headnorm, batch, seq, length):
    t_rows = proj.shape[0]
    nc = seq // length
    col = lambda k: pl.BlockSpec((length, BRANCH_W), lambda b, c: (b * nc + c, k))
    const = lambda shape: pl.BlockSpec(shape, lambda b, c: (0, 0))
    return pl.pallas_call(
        functools.partial(_mlstm_kernel, length=length),
        grid=(batch, nc),
        in_specs=[
            col(3), col(4), col(5), col(6),
            pl.BlockSpec((length, LANES_V7X), lambda b, c: (b * nc + c, 0)),
            const((1, LANES_V7X)),
            const((CONV_W, 2 * BRANCH_W)),
            const((1, 2 * BRANCH_W)),
            const((1, BRANCH_W)),
        ],
        out_specs=pl.BlockSpec((length, BRANCH_W), lambda b, c: (b * nc + c, 0)),
        out_shape=jax.ShapeDtypeStruct((t_rows, BRANCH_W), BF16),
        scratch_shapes=[
            pltpu.VMEM((ML_HEADS, HEAD_W, HEAD_W), F32),
            pltpu.VMEM((ML_HEADS, HEAD_W), F32),
            pltpu.VMEM((ML_HEADS, LANES_V7X), F32),
            pltpu.VMEM((SUBLANES_V7X + length, BRANCH_W), F32),
            pltpu.VMEM((SUBLANES_V7X + length, BRANCH_W), F32),
        ],
        compiler_params=_cparams(("parallel", "arbitrary")),
        name="mlstm",
    )(proj, proj, proj, proj, gates, gbias, conv_w, conv_b, headnorm)


def _merge_out_kernel(*refs, n_gate):
    a_ref, b_ref, wa_ref, wb_ref = refs[:4]
    ga_refs = refs[4:4 + n_gate]
    gb_refs = refs[4 + n_gate:4 + 2 * n_gate]
    x_ref, w_ref, g_ref, x1t_ref, xnt_ref = refs[4 + 2 * n_gate:]
    pa = jnp.dot(a_ref[...], wa_ref[...], preferred_element_type=F32)
    pb = jnp.dot(b_ref[...], wb_ref[...], preferred_element_type=F32)
    ga = jnp.concatenate([r[...] for r in ga_refs], axis=1).astype(F32)
    gb = jnp.concatenate([r[...] for r in gb_refs], axis=1).astype(F32)
    merged = (ga * pa + gb * pb).astype(BF16)
    x1 = x_ref[...] + jnp.dot(merged, w_ref[...], preferred_element_type=F32)
    ms = jnp.mean(x1 * x1, axis=-1, keepdims=True)
    xn = x1 * lax.rsqrt(ms + EPS) * g_ref[...]
    x1t_ref[...] = x1.T
    xnt_ref[...] = xn.T.astype(BF16)


def _merge_out(branch_a, branch_b, wa, wb, proj, x2, w_out, g, tm):
    t_rows, d = x2.shape
    gw = min(BRANCH_W, d)
    n_gate = d // gw
    ga0 = N_PLAIN_COLS // gw
    gb0 = (N_PLAIN_COLS + d) // gw
    once = dict(pipeline_mode=pl.Buffered(1))
    gate_specs = ([pl.BlockSpec((tm, gw), functools.partial(lambda i, c: (i, c), c=ga0 + k)) for k in range(n_gate)]
                  + [pl.BlockSpec((tm, gw), functools.partial(lambda i, c: (i, c), c=gb0 + k)) for k in range(n_gate)])
    return pl.pallas_call(
        functools.partial(_merge_out_kernel, n_gate=n_gate),
        grid=(t_rows // tm,),
        in_specs=[
            pl.BlockSpec((tm, BRANCH_W), lambda i: (i, 0)),
            pl.BlockSpec((tm, BRANCH_W), lambda i: (i, 0)),
            pl.BlockSpec((BRANCH_W, d), lambda i: (0, 0), **once),
            pl.BlockSpec((BRANCH_W, d), lambda i: (0, 0), **once),
            *gate_specs,
            pl.BlockSpec((tm, d), lambda i: (i, 0)),
            pl.BlockSpec((d, d), lambda i: (0, 0), **once),
            pl.BlockSpec((1, d), lambda i: (0, 0)),
        ],
        out_specs=[
            pl.BlockSpec((d, tm), lambda i: (0, i)),
            pl.BlockSpec((d, tm), lambda i: (0, i)),
        ],
        out_shape=[
            jax.ShapeDtypeStruct((d, t_rows), F32),
            jax.ShapeDtypeStruct((d, t_rows), BF16),
        ],
        compiler_params=_cparams(("parallel",)),
        name="merge_out",
    )(branch_a, branch_b, wa, wb, *([proj] * (2 * n_gate)), x2, w_out, g)


INT_MIN = -2 ** 31


def _sortable(x):
    bits = lax.bitcast_convert_type(x, jnp.int32)
    return bits ^ ((bits >> 31) & 0x7FFFFFFF)


def _unsortable(k):
    return lax.bitcast_convert_type(k ^ ((k >> 31) & 0x7FFFFFFF), F32)


def _top16_ranks(s_ref, shape3, key_ref, val_ref):
    def load_keys():
        key_ref[...] = _sortable(s_ref[...].reshape(shape3) + 0.0)

    def run(extract):
        load_keys()
        lax.fori_loop(0, TOPK, extract, jnp.max(key_ref[...], axis=0))

    def take_all(r, mx):
        val_ref[r] = mx
        cur = key_ref[...]
        cur = jnp.where(cur == mx[None], INT_MIN + 1 + r, cur)
        key_ref[...] = cur
        return jnp.max(cur, axis=0)

    def take_first(r, mx):
        val_ref[r] = mx
        cur = key_ref[...]
        kidx = lax.broadcasted_iota(jnp.int32, shape3, 0)
        first = jnp.min(jnp.where(cur == mx[None], kidx, N_KEYS), axis=0)
        cur = jnp.where(kidx == first[None], INT_MIN + 1 + r, cur)
        key_ref[...] = cur
        return jnp.max(cur, axis=0)

    run(take_all)
    taken = jnp.sum((key_ref[...] <= INT_MIN + TOPK).astype(F32), axis=0)

    @pl.when(jnp.max(taken) > float(TOPK))
    def _():
        run(take_first)

    cur = key_ref[...]
    return jnp.where(cur <= INT_MIN + TOPK, cur - INT_MIN, 0)


def _staircase(a, b):
    ridx = lax.broadcasted_iota(jnp.int32, a.shape, 0)
    top = a[0] + b[0]
    cnt = jnp.zeros(a.shape, jnp.int32)
    nxt = jnp.broadcast_to(b[0][None], a.shape)
    z = jnp.zeros(top.shape, F32)
    for _ in range(TOPK):
        f = a + nxt
        mx = jnp.max(f, axis=0)
        first = jnp.min(jnp.where(f == mx[None], ridx, TOPK), axis=0)
        sel = ridx == first[None]
        cnt = jnp.where(sel, cnt + 1, cnt)
        taken = jnp.max(jnp.where(sel, cnt, 0), axis=0)
        b_next = jnp.full(top.shape, NEG_INF, F32)
        for c in range(1, TOPK):
            b_next = jnp.where(taken == c, b[c], b_next)
        nxt = jnp.where(sel, b_next[None], nxt)
        z = z + jnp.exp(mx - top)
    return cnt, z


def _peer_rank_kernel(xnt_ref, wq_ref, skb_ref, perm_ref, e0_ref, n0_ref, e1_ref, r1_ref,
                      qt_ref, s_ref, key_ref, rank0_ref, a_ref, b_ref):
    tm = xnt_ref.shape[1]
    rows = N_KEYS * PEER_HEADS
    shape3 = (N_KEYS, PEER_HEADS, tm)
    qt_ref[...] = jnp.dot(wq_ref[...], xnt_ref[...], preferred_element_type=F32).astype(BF16)
    for c in range(2):
        s_ref[c] = jnp.dot(skb_ref[c], qt_ref[c * rows:(c + 1) * rows, :], preferred_element_type=F32)

    rank0_ref[...] = _top16_ranks(s_ref.at[0], shape3, key_ref, a_ref)
    rank1 = _top16_ranks(s_ref.at[1], shape3, key_ref, b_ref)
    a = _unsortable(a_ref[...])
    b = _unsortable(b_ref[...])
    cnt, z = _staircase(a, b)

    rank0 = rank0_ref[...]
    n0 = jnp.zeros(shape3, jnp.int32)
    for r in range(TOPK):
        n0 = jnp.where(rank0 == r + 1, cnt[r][None], n0)
    n0_ref[...] = n0.astype(F32)
    e0_ref[...] = jnp.exp(s_ref[0].reshape(shape3) - a[0][None]) * (1.0 / z)[None]

    r1 = jnp.where(rank1 == 0, 2 * TOPK, rank1).astype(F32).astype(BF16).reshape(rows, tm)
    e1 = jnp.exp(s_ref[1].reshape(shape3) - b[0][None]).astype(BF16).reshape(rows, tm)
    out3 = (PEER_HEADS, N_KEYS, tm)
    r1_ref[...] = jnp.dot(perm_ref[...], r1, preferred_element_type=F32).astype(BF16).reshape(out3)
    e1_ref[...] = jnp.dot(perm_ref[...], e1, preferred_element_type=F32).astype(BF16).reshape(out3)


def _peer_rank(xnt, wq_t, skb, perm, tm):
    d, t_rows = xnt.shape
    qd = wq_t.shape[0]
    rows = N_KEYS * PEER_HEADS
    const = dict(pipeline_mode=pl.Buffered(1))
    return pl.pallas_call(
        _peer_rank_kernel,
        grid=(t_rows // tm,),
        in_specs=[
            pl.BlockSpec((d, tm), lambda i: (0, i)),
            pl.BlockSpec((qd, d), lambda i: (0, 0), **const),
            pl.BlockSpec((2, rows, rows), lambda i: (0, 0, 0), **const),
            pl.BlockSpec((rows, rows), lambda i: (0, 0), **const),
        ],
        out_specs=[
            pl.BlockSpec((N_KEYS, PEER_HEADS, tm), lambda i: (0, 0, i)),
            pl.BlockSpec((N_KEYS, PEER_HEADS, tm), lambda i: (0, 0, i)),
            pl.BlockSpec((PEER_HEADS, N_KEYS, tm), lambda i: (0, 0, i)),
            pl.BlockSpec((PEER_HEADS, N_KEYS, tm), lambda i: (0, 0, i)),
        ],
        out_shape=[
            jax.ShapeDtypeStruct((N_KEYS, PEER_HEADS, t_rows), F32),
            jax.ShapeDtypeStruct((N_KEYS, PEER_HEADS, t_rows), F32),
            jax.ShapeDtypeStruct((PEER_HEADS, N_KEYS, t_rows), BF16),
            jax.ShapeDtypeStruct((PEER_HEADS, N_KEYS, t_rows), BF16),
        ],
        scratch_shapes=[
            pltpu.VMEM((qd, tm), BF16),
            pltpu.VMEM((2, rows, tm), F32),
            pltpu.VMEM((N_KEYS, PEER_HEADS, tm), jnp.int32),
            pltpu.VMEM((N_KEYS, PEER_HEADS, tm), jnp.int32),
            pltpu.VMEM((TOPK, PEER_HEADS, tm), jnp.int32),
            pltpu.VMEM((TOPK, PEER_HEADS, tm), jnp.int32),
        ],
        compiler_params=_cparams(("parallel",)),
        name="peer_rank",
    )(xnt, wq_t, skb, perm)


def _peer_mix_kernel(xnt_ref, x1t_ref, u_ref, vt_ref, e0_ref, n0_ref, e1_ref, r1_ref, g_ref,
                     y_ref, acc_ref, a0_ref, a1_ref, *, eb, n_blocks):
    si = pl.program_id(1)
    n_i = eb // N_KEYS
    tm = acc_ref.shape[1]

    @pl.when(si == 0)
    def _():
        acc_ref[...] = jnp.zeros_like(acc_ref)
        a1_ref[...] = jnp.zeros_like(a1_ref)

    def step(cur_ref, prev_ref):
        blk = jnp.maximum(si - 1, 0)
        tiles = []
        for ii in range(n_i):
            i = blk * n_i + ii
            shape = (N_KEYS, tm)
            wgt = jnp.zeros(shape, BF16)
            n_rows = n0_ref[i]
            e_rows = e0_ref[i]
            for h in range(PEER_HEADS):
                n_tile = jnp.broadcast_to(n_rows[h:h + 1], shape).astype(BF16)
                e_tile = jnp.broadcast_to(e_rows[h:h + 1], shape).astype(BF16)
                wgt = wgt + e_tile * jnp.where(r1_ref[h] <= n_tile, e1_ref[h], jnp.zeros(shape, BF16))
            rows = slice(ii * N_KEYS, (ii + 1) * N_KEYS)
            tiles.append(jax.nn.gelu(prev_ref[rows, :].astype(BF16)) * wgt)
        act = jnp.concatenate(tiles, axis=0)
        cur_ref[...] = jnp.dot(u_ref[...], xnt_ref[...], preferred_element_type=F32)
        acc_ref[...] += jnp.dot(vt_ref[...], act, preferred_element_type=F32)

    parity = lax.rem(si, 2)

    @pl.when(parity == 0)
    def _():
        step(a0_ref, a1_ref)

    @pl.when(parity == 1)
    def _():
        step(a1_ref, a0_ref)

    @pl.when(si == n_blocks)
    def _():
        x2 = (x1t_ref[...] + acc_ref[...]).T
        ms = jnp.mean(x2 * x2, axis=-1, keepdims=True)
        y_ref[...] = x2 * lax.rsqrt(ms + EPS) * g_ref[...]


def _peer_mix(xnt, x1t, u, vt, e0, n0, e1, r1, g, tm, eb):
    d, t_rows = xnt.shape
    nb = u.shape[0] // eb
    once = dict(pipeline_mode=pl.Buffered(1))
    head_spec = pl.BlockSpec((PEER_HEADS, N_KEYS, tm), lambda t, s: (0, 0, t))
    key_spec = pl.BlockSpec((N_KEYS, PEER_HEADS, tm), lambda t, s: (0, 0, t), **once)
    return pl.pallas_call(
        functools.partial(_peer_mix_kernel, eb=eb, n_blocks=nb),
        grid=(t_rows // tm, nb + 1),
        in_specs=[
            pl.BlockSpec((d, tm), lambda t, s: (0, t)),
            pl.BlockSpec((d, tm), lambda t, s: (0, t), **once),
            pl.BlockSpec((eb, d), lambda t, s: (jnp.minimum(s, nb - 1), 0)),
            pl.BlockSpec((d, eb), lambda t, s: (0, jnp.maximum(s - 1, 0))),
            key_spec, key_spec, head_spec, head_spec,
            pl.BlockSpec((1, d), lambda t, s: (0, 0)),
        ],
        out_specs=pl.BlockSpec((tm, d), lambda t, s: (t, 0)),
        out_shape=jax.ShapeDtypeStruct((t_rows, d), F32),
        scratch_shapes=[pltpu.VMEM((d, tm), F32), pltpu.VMEM((eb, tm), F32), pltpu.VMEM((eb, tm), F32)],
        compiler_params=_cparams(("parallel", "arbitrary")),
        name="peer_mix",
    )(xnt, x1t, u, vt, e0, n0, e1, r1, g)


def kernel(x, norm_mix, w_in, conv_w, conv_b, ml_gate_bias, da_lambda, da_subln, ml_headnorm,
           w_branch_a, w_branch_b, w_out, norm_ffn, peer_query, peer_subkeys, expert_u, expert_v,
           norm_final):
    batch, seq, d = x.shape
    t_rows = batch * seq
    assert norm_mix.shape[0] == 1, "single-layer trunk"
    assert seq % CHUNK == 0 and d % LANES_V7X == 0
    ts = _tiles(t_rows, seq, d)
    x2 = x.reshape(t_rows, d)

    w = w_in[0]
    gate0 = N_PLAIN_COLS + N_GATE_LOGITS
    w_cat = jnp.concatenate([w[:, :N_PLAIN_COLS], w[:, gate0:]], axis=1).astype(BF16)
    w_if = jnp.pad(w[:, N_PLAIN_COLS:gate0], ((0, 0), (0, LANES_V7X - N_GATE_LOGITS))).astype(BF16)
    gbias = jnp.pad(ml_gate_bias[0], (0, LANES_V7X - N_GATE_LOGITS)).reshape(1, LANES_V7X)
    slopes = 2.0 ** (-8.0 * jnp.arange(1, DA_HEADS + 1, dtype=F32) / DA_HEADS)
    slopes = jnp.broadcast_to(slopes[:, None, None], (DA_HEADS, 1, LANES_V7X))

    proj, gates = _inproj(x2, norm_mix[0].reshape(1, d), w_cat, w_if, ts["proj_tm"], ts["proj_tn"])
    branch_a = _attention(proj, slopes, da_lambda[0], da_subln[0].reshape(HEAD_W, 1), batch, seq, ts["attn_tq"])
    branch_b = _mlstm(proj, gates, gbias, conv_w[0], conv_b[0].reshape(1, -1),
                      ml_headnorm[0].reshape(1, BRANCH_W), batch, seq, ts["ml_len"])
    x1t, xnt = _merge_out(branch_a, branch_b, w_branch_a[0].astype(BF16), w_branch_b[0].astype(BF16), proj,
                          x2, w_out[0].astype(BF16), norm_ffn[0].reshape(1, d), ts["merge_out_tm"])
    half_dim = peer_subkeys.shape[-1]
    wq_t = peer_query[0].reshape(d, PEER_HEADS, 2, half_dim).transpose(2, 1, 3, 0)
    wq_t = wq_t.reshape(2 * PEER_HEADS * half_dim, d).astype(BF16)
    eye = jnp.eye(PEER_HEADS, dtype=peer_subkeys.dtype)
    skb = jnp.einsum("hckd,hg->ckhgd", peer_subkeys[0], eye)
    skb = skb.reshape(2, N_KEYS * PEER_HEADS, PEER_HEADS * half_dim).astype(BF16)
    src_row = jnp.arange(N_KEYS * PEER_HEADS)
    perm = (src_row[None, :] == ((src_row % N_KEYS) * PEER_HEADS + src_row // N_KEYS)[:, None]).astype(BF16)
    e0, n0, e1, r1 = _peer_rank(xnt, wq_t, skb, perm, ts["peer_tm"])
    y = _peer_mix(xnt, x1t, expert_u[0].astype(BF16), expert_v[0].T.astype(BF16), e0, n0, e1, r1,
                  norm_final.reshape(1, d), ts["peer_tm"], ts["peer_eb"])
    return y.reshape(batch, seq, d)
```
